```python
import jax, jax.numpy as jnp
from jax import lax
import numpy as np

D_MODEL = 1024
BATCH = 8
SEQ = 8192
DEPTH = 1

CHUNK = 64
Q_BLOCK = 128
D_MIX = D_MODEL
MLA_HEADS = 4
MLA_NOPE = 128
MLA_ROPE = 64
MLA_V = 128
MLA_WIDTH = MLA_HEADS * MLA_V
Q_LORA = 384
KV_LORA = 256
ROPE_BASE = 10000.0
RWKV_HEAD = 64
RWKV_WIDTH = D_MIX - MLA_WIDTH
RWKV_HEADS = RWKV_WIDTH // RWKV_HEAD
DECAY_LORA = 64
AAA_LORA = 64
GATE_LORA = 128
MLA_COLS = Q_LORA + KV_LORA + MLA_ROPE
RWKV_COLS = 3 * RWKV_WIDTH + DECAY_LORA + AAA_LORA + GATE_LORA
IN_COLS = MLA_COLS + RWKV_COLS
D_FF = 2816
NORM_EPS = 1e-6
GN_EPS = 64e-5
N_MOD = 9

kernel_name = "hybrid_mla_rwkv7_macaron_adaln"


def rmsnorm(x, g, eps=NORM_EPS):
    x32 = x.astype(jnp.float32)
    y = x32 * lax.rsqrt(jnp.mean(x32 * x32, axis=-1, keepdims=True) + eps)
    return (y * g.astype(jnp.float32)).astype(x.dtype)


def modulate(u, shift, scale):
    return u * (1.0 + scale[:, None, :]) + shift[:, None, :]


def swiglu(u, w_gate, w_up, w_down):
    return (jax.nn.silu(u @ w_gate) * (u @ w_up)) @ w_down


def apply_rot(x, cos, sin):
    half = x.shape[-1] // 2
    x32 = x.astype(jnp.float32)
    x1, x2 = x32[..., :half], x32[..., half:]
    return jnp.concatenate([x1 * cos - x2 * sin, x2 * cos + x1 * sin], axis=-1).astype(x.dtype)


def mla_group(p_q, p_kv, p_kr, cos, sin, q_norm_g, w_uq, kv_norm_g, w_ukv):
    B, S, _ = p_q.shape
    q = (rmsnorm(p_q, q_norm_g) @ w_uq).reshape(B, S, MLA_HEADS, MLA_NOPE + MLA_ROPE)
    q_nope = q[..., :MLA_NOPE]
    q_rope = apply_rot(q[..., MLA_NOPE:], cos[:, :, None, :], sin[:, :, None, :])
    kv = (rmsnorm(p_kv, kv_norm_g) @ w_ukv).reshape(B, S, MLA_HEADS, MLA_NOPE + MLA_V)
    k_nope, v = kv[..., :MLA_NOPE], kv[..., MLA_NOPE:]
    k_rope = apply_rot(p_kr, cos, sin)
    nb = S // Q_BLOCK
    qn_b = q_nope.reshape(B, nb, Q_BLOCK, MLA_HEADS, MLA_NOPE).transpose(1, 0, 2, 3, 4)
    qr_b = q_rope.reshape(B, nb, Q_BLOCK, MLA_HEADS, MLA_ROPE).transpose(1, 0, 2, 3, 4)
    key_chunk = jnp.arange(S) // CHUNK
    scale = (MLA_NOPE + MLA_ROPE) ** -0.5

    def block(args):
        qn, qr, i = args
        s = (jnp.einsum('bqhd,bkhd->bhqk', qn, k_nope)
             + jnp.einsum('bqhd,bkd->bhqk', qr, k_rope)).astype(jnp.float32) * scale
        q_chunk = (i * Q_BLOCK + jnp.arange(Q_BLOCK)) // CHUNK
        mask = key_chunk[None, :] <= q_chunk[:, None]
        s = jnp.where(mask[None, None], s, -jnp.inf)
        pr = jax.nn.softmax(s, axis=-1).astype(v.dtype)
        return jnp.einsum('bhqk,bkhd->bqhd', pr, v)

    o = lax.map(block, (qn_b, qr_b, jnp.arange(nb)))
    return o.transpose(1, 0, 2, 3, 4).reshape(B, S, MLA_WIDTH)


def rwkv7_group(p, shift_mix, w0, w2, a0, a2, g2, k_k, k_a, r_k, ln_w, ln_b):
    B, S, _ = p.shape
    C, H, N = RWKV_WIDTH, RWKV_HEADS, RWKV_HEAD
    f32 = jnp.float32
    p_prev = jnp.pad(p, ((0, 0), (1, 0), (0, 0)))[:, :-1]
    p = p + (p_prev - p) * shift_mix
    r, k, v = p[..., :C], p[..., C:2 * C], p[..., 2 * C:3 * C]
    o = 3 * C
    xw = p[..., o:o + DECAY_LORA]
    xa = p[..., o + DECAY_LORA:o + DECAY_LORA + AAA_LORA]
    xg = p[..., o + DECAY_LORA + AAA_LORA:]
    w_log = -jax.nn.softplus(-(w0 + jnp.tanh(xw) @ w2)) - 0.5
    decay = jnp.exp(-jnp.exp(w_log.astype(f32)))
    a = jax.nn.sigmoid(a0 + xa @ a2)
    g = jax.nn.sigmoid(xg) @ g2
    kk = (k * k_k).astype(f32).reshape(B, S, H, N)
    kk = kk / jnp.maximum(jnp.sqrt(jnp.sum(kk * kk, axis=-1, keepdims=True)), 1e-12)
    k = k * (1.0 + (a - 1.0) * k_a)
    heads = lambda t: t.astype(f32).reshape(B, S, H, N)
    r_h, k_h, v_h, a_h = heads(r), heads(k), heads(v), heads(a)
    tm = lambda t: jnp.moveaxis(t.reshape(B, S, H, N), 1, 0)

    def step(state, inp):
        r_t, w_t, k_t, v_t, aa_t, bb_t = inp
        sa = jnp.einsum('bhvk,bhk->bhv', state, aa_t)
        state = (state * w_t[:, :, None, :] + sa[..., None] * bb_t[:, :, None, :]
                 + v_t[..., None] * k_t[:, :, None, :])
        return state, jnp.einsum('bhvk,bhk->bhv', state, r_t)

    s0 = jnp.zeros((B, H, N, N), f32)
    _, y = lax.scan(step, s0, (tm(r_h), tm(decay), tm(k_h), tm(v_h), tm(-kk), tm(kk * a_h)))
    y = jnp.moveaxis(y, 0, 1)
    mu = jnp.mean(y, axis=-1, keepdims=True)
    var = jnp.mean(jnp.square(y - mu), axis=-1, keepdims=True)
    y = ((y - mu) * lax.rsqrt(var + GN_EPS)).reshape(B, S, C) * ln_w.astype(f32) + ln_b.astype(f32)
    bonus = jnp.sum(r_h * k_h * r_k.astype(f32), axis=-1, keepdims=True) * v_h
    out = (y + bonus.reshape(B, S, C)) * g.astype(f32)
    return out.astype(p.dtype)


def setup_inputs(seed: int = 0) -> dict:
    key = jax.random.key(seed)
    ks = iter(jax.random.split(key, 48))
    L, D, C = DEPTH, D_MODEL, RWKV_WIDTH

    def nrm(shape, scale):
        return scale * jax.random.normal(next(ks), shape, jnp.float32)

    def gain(shape):
        return 1.0 + nrm(shape, 0.02)

    def unif(shape, lo, hi):
        return jax.random.uniform(next(ks), shape, jnp.float32, lo, hi)

    x = nrm((BATCH, SEQ, D), 1.0)
    c = nrm((BATCH, D), 1.0)
    offset = jax.random.randint(next(ks), (BATCH, 1), 0, 4096, dtype=jnp.int32)
    positions = offset + jnp.arange(SEQ, dtype=jnp.int32)[None, :]
    return {
        "x": x,
        "c": c,
        "positions": positions,
        "w_mod": nrm((L, D, N_MOD * D), 0.5 * D ** -0.5),
        "b_mod": nrm((L, N_MOD * D), 0.02),
        "ffn1_norm_g": gain((L, D)),
        "ffn1_w_gate": nrm((L, D, D_FF), D ** -0.5),
        "ffn1_w_up": nrm((L, D, D_FF), D ** -0.5),
        "ffn1_w_down": nrm((L, D_FF, D), D_FF ** -0.5),
        "mix_norm_g": gain((L, D)),
        "w_in": nrm((L, D, IN_COLS), D ** -0.5),
        "q_norm_g": gain((L, Q_LORA)),
        "w_uq": nrm((L, Q_LORA, MLA_HEADS * (MLA_NOPE + MLA_ROPE)), Q_LORA ** -0.5),
        "kv_norm_g": gain((L, KV_LORA)),
        "w_ukv": nrm((L, KV_LORA, MLA_HEADS * (MLA_NOPE + MLA_V)), KV_LORA ** -0.5),
        "attn_out_norm_g": gain((L, MLA_WIDTH)),
        "rwkv_shift_mix": unif((L, RWKV_COLS), 0.0, 1.0),
        "rwkv_w0": unif((L, C), -5.5, 0.5),
        "rwkv_w2": nrm((L, DECAY_LORA, C), 0.1 * DECAY_LORA ** -0.5),
        "rwkv_a0": nrm((L, C), 0.1),
        "rwkv_a2": nrm((L, AAA_LORA, C), 0.5 * AAA_LORA ** -0.5),
        "rwkv_g2": nrm((L, GATE_LORA, C), GATE_LORA ** -0.5),
        "rwkv_k_k": 0.85 + nrm((L, C), 0.05),
        "rwkv_k_a": 1.0 + nrm((L, C), 0.05),
        "rwkv_r_k": nrm((L, RWKV_HEADS, RWKV_HEAD), 0.1),
        "rwkv_ln_w": gain((L, C)),
        "rwkv_ln_b": nrm((L, C), 0.02),
        "w_out": nrm((L, D_MIX, D), D_MIX ** -0.5),
        "ffn2_norm_g": gain((L, D)),
        "ffn2_w_gate": nrm((L, D, D_FF), D ** -0.5),
        "ffn2_w_up": nrm((L, D, D_FF), D ** -0.5),
        "ffn2_w_down": nrm((L, D_FF, D), D_FF ** -0.5),
        "final_norm_g": gain((D,)),
    }


def reference(x, c, positions, w_mod, b_mod, ffn1_norm_g, ffn1_w_gate, ffn1_w_up, ffn1_w_down,
              mix_norm_g, w_in, q_norm_g, w_uq, kv_norm_g, w_ukv, attn_out_norm_g,
              rwkv_shift_mix, rwkv_w0, rwkv_w2, rwkv_a0, rwkv_a2, rwkv_g2, rwkv_k_k, rwkv_k_a,
              rwkv_r_k, rwkv_ln_w, rwkv_ln_b, w_out, ffn2_norm_g, ffn2_w_gate, ffn2_w_up,
              ffn2_w_down, final_norm_g):
    half = MLA_ROPE // 2
    inv_freq = ROPE_BASE ** (-jnp.arange(half, dtype=jnp.float32) / half)
    ang = positions.astype(jnp.float32)[..., None] * inv_freq
    cos, sin = jnp.cos(ang), jnp.sin(ang)
    c_act = jax.nn.silu(c)
    h = x
    for l in range(DEPTH):
        mod = c_act @ w_mod[l] + b_mod[l]
        sh1, sc1, gt1, sh2, sc2, gt2, sh3, sc3, gt3 = jnp.split(mod, N_MOD, axis=-1)
        u = modulate(rmsnorm(h, ffn1_norm_g[l]), sh1, sc1)
        h = h + 0.5 * gt1[:, None, :] * swiglu(u, ffn1_w_gate[l], ffn1_w_up[l], ffn1_w_down[l])
        u = modulate(rmsnorm(h, mix_norm_g[l]), sh2, sc2)
        proj = u @ w_in[l]
        p_q = proj[..., :Q_LORA]
        p_kv = proj[..., Q_LORA:Q_LORA + KV_LORA]
        p_kr = proj[..., Q_LORA + KV_LORA:MLA_COLS]
        p_rw = proj[..., MLA_COLS:]
        y_a = mla_group(p_q, p_kv, p_kr, cos, sin, q_norm_g[l], w_uq[l], kv_norm_g[l], w_ukv[l])
        y_a = rmsnorm(y_a, attn_out_norm_g[l])
        y_b = rwkv7_group(p_rw, rwkv_shift_mix[l], rwkv_w0[l], rwkv_w2[l], rwkv_a0[l], rwkv_a2[l],
                          rwkv_g2[l], rwkv_k_k[l], rwkv_k_a[l], rwkv_r_k[l], rwkv_ln_w[l], rwkv_ln_b[l])
        y = jnp.concatenate([y_a, y_b], axis=-1) @ w_out[l]
        h = h + gt2[:, None, :] * y
        u = modulate(rmsnorm(h, ffn2_norm_g[l]), sh3, sc3)
        h = h + 0.5 * gt3[:, None, :] * swiglu(u, ffn2_w_gate[l], ffn2_w_up[l], ffn2_w_down[l])
    return rmsnorm(h, final_norm_g)
```

```python
import functools

import jax
import jax.numpy as jnp
from jax import lax
from jax.experimental import pallas as pl
from jax.experimental.pallas import tpu as pltpu

F32 = jnp.float32
BF16 = jnp.bfloat16

MLA_HEADS = 4
MLA_NOPE = 128
MLA_ROPE = 64
MLA_V = 128
Q_LORA = 384
KV_LORA = 256
ROPE_BASE = 10000.0
RWKV_HEAD = 64
DECAY_LORA = 64
AAA_LORA = 64
GATE_LORA = 128
NORM_EPS = 1e-6
GN_EPS = 64e-5
N_MOD = 9
ATTN_CHUNK = 64

LANES = 128
VMEM_LIMIT_BYTES = 56 * 1024 * 1024

FFN_ROWS = 512
ATTN_BLOCK = 256
RWKV_CHUNK = 64
INV_BLOCK = 16


def _dot(a, b):
    return jnp.dot(a.astype(BF16), b.astype(BF16), preferred_element_type=F32)


def _dot_nt(a, b):
    return lax.dot_general(a.astype(BF16), b.astype(BF16), (((1,), (1,)), ((), ())),
                           preferred_element_type=F32)


def _dot_tn(a, b):
    return lax.dot_general(a.astype(BF16), b.astype(BF16), (((0,), (0,)), ((), ())),
                           preferred_element_type=F32)


def _split2(x):
    hi = x.astype(BF16)
    lo = (x - hi.astype(F32)).astype(BF16)
    return hi, lo


def _split3(x):
    hi = x.astype(BF16)
    r1 = x - hi.astype(F32)
    mid = r1.astype(BF16)
    lo = (r1 - mid.astype(F32)).astype(BF16)
    return hi, mid, lo


def _sigmoid(x):
    return 1.0 / (1.0 + jnp.exp(-x))


def _rms(x, g):
    return x * lax.rsqrt(jnp.mean(x * x, axis=-1, keepdims=True) + NORM_EPS) * g


def _const_spec(shape):
    nd = len(shape)
    return pl.BlockSpec(shape, lambda *_: (0,) * nd, pipeline_mode=pl.Buffered(1))


def _params(semantics):
    return pltpu.CompilerParams(dimension_semantics=semantics, vmem_limit_bytes=VMEM_LIMIT_BYTES)


def _mod_kernel(c_ref, w_ref, b_ref, o_ref):
    c = c_ref[...]
    c_act = c * _sigmoid(c)
    o_ref[...] = _dot(c_act, w_ref[...]) + b_ref[...]


def _mod(c, w_mod, b_mod):
    bsz, d = c.shape
    n = w_mod.shape[1]
    tn = d
    return pl.pallas_call(
        _mod_kernel,
        grid=(n // tn,),
        in_specs=[pl.BlockSpec((bsz, d), lambda j: (0, 0)),
                  pl.BlockSpec((d, tn), lambda j: (0, j)),
                  pl.BlockSpec((1, tn), lambda j: (0, j))],
        out_specs=pl.BlockSpec((bsz, tn), lambda j: (0, j)),
        out_shape=jax.ShapeDtypeStruct((bsz, n), F32),
        compiler_params=_params(("arbitrary",)),
        name="mod",
    )(c, w_mod, b_mod.reshape(1, n))


def _ffn_kernel(*refs, mod_rows, pre_mix, final_norm):
    it = iter(refs)
    h_ref, mod_ref, ng_ref, wg_ref, wu_ref, wd_ref = (next(it) for _ in range(6))
    if pre_mix:
        ya_ref, yb_ref, ang_ref, woa_ref, wob_ref = (next(it) for _ in range(5))
    if final_norm:
        fng_ref = next(it)
    o_ref = next(it)

    h = h_ref[0]
    mod = mod_ref[0]
    row = lambda i: mod[i:i + 1, :]
    if pre_mix:
        ya = _rms(ya_ref[0], ang_ref[...])
        y = _dot(ya, woa_ref[...]) + _dot(yb_ref[0], wob_ref[...])
        h = h + row(mod_rows[3]) * y
    sh, sc, gt = row(mod_rows[0]), row(mod_rows[1]), row(mod_rows[2])
    u = (_rms(h, ng_ref[...]) * (1.0 + sc) + sh).astype(BF16)
    g = jnp.dot(u, wg_ref[...], preferred_element_type=F32)
    up = jnp.dot(u, wu_ref[...], preferred_element_type=F32)
    act = (g * _sigmoid(g) * up).astype(BF16)
    f = jnp.dot(act, wd_ref[...], preferred_element_type=F32)
    h = h + 0.5 * gt * f
    if final_norm:
        h = _rms(h, fng_ref[...])
    o_ref[0] = h


def _ffn(h, mod, ng, wg, wu, wd, *, mod_rows, mix=None, final_g=None):
    bsz, s, d = h.shape
    f = wg.shape[1]
    tm = min(FFN_ROWS, s)
    row_spec = lambda w: pl.BlockSpec((1, tm, w), lambda b, i: (b, i, 0))
    in_specs = [row_spec(d),
                pl.BlockSpec((1, N_MOD, d), lambda b, i: (b, 0, 0)),
                _const_spec((1, d)), _const_spec((d, f)), _const_spec((d, f)), _const_spec((f, d))]
    args = [h, mod, ng.reshape(1, d), wg, wu, wd]
    if mix is not None:
        ya, yb, ang, woa, wob = mix
        wa, wb = ya.shape[-1], yb.shape[-1]
        in_specs += [row_spec(wa), row_spec(wb), _const_spec((1, wa)),
                     _const_spec((wa, d)), _const_spec((wb, d))]
        args += [ya, yb, ang.reshape(1, wa), woa, wob]
    if final_g is not None:
        in_specs.append(_const_spec((1, d)))
        args.append(final_g.reshape(1, d))
    kern = functools.partial(_ffn_kernel, mod_rows=mod_rows, pre_mix=mix is not None,
                             final_norm=final_g is not None)
    return pl.pallas_call(
        kern,
        grid=(bsz, s // tm),
        in_specs=in_specs,
        out_specs=row_spec(d),
        out_shape=jax.ShapeDtypeStruct((bsz, s, d), F32),
        compiler_params=_params(("parallel", "parallel")),
        name="ffn_mix" if mix is not None else "ffn",
    )(*args)


def _inproj_kernel(h_ref, mod_ref, ng_ref, w_ref, pm_ref, pr_ref, *, n_mla):
    mod = mod_ref[0]
    u = _rms(h_ref[0], ng_ref[...]) * (1.0 + mod[4:5, :]) + mod[3:4, :]
    p = _dot(u, w_ref[...])
    pm_ref[0] = p[:, :n_mla]
    pr_ref[0] = p[:, n_mla:]


def _inproj(h, mod, ng, w, n_mla):
    bsz, s, d = h.shape
    n = w.shape[1]
    tm = min(FFN_ROWS, s)
    row_spec = lambda wd_: pl.BlockSpec((1, tm, wd_), lambda b, i: (b, i, 0))
    return pl.pallas_call(
        functools.partial(_inproj_kernel, n_mla=n_mla),
        grid=(bsz, s // tm),
        in_specs=[row_spec(d), pl.BlockSpec((1, N_MOD, d), lambda b, i: (b, 0, 0)),
                  _const_spec((1, d)), _const_spec((d, n))],
        out_specs=[row_spec(n_mla), row_spec(n - n_mla)],
        out_shape=[jax.ShapeDtypeStruct((bsz, s, n_mla), F32),
                   jax.ShapeDtypeStruct((bsz, s, n - n_mla), F32)],
        compiler_params=_params(("parallel", "parallel")),
        name="inproj",
    )(h, mod, ng.reshape(1, d), w)


def _mlaprep_kernel(p_ref, cs_ref, qg_ref, kvg_ref, wq_ref, wkv_ref, q_ref, k_ref, v_ref, *, scale):
    p = p_ref[0]
    cs = cs_ref[0]
    nh, nope = MLA_HEADS, MLA_NOPE
    lane = lax.broadcasted_iota(jnp.int32, cs.shape, 1)
    low = lane < MLA_ROPE

    def rope(xpair):
        prod = xpair * cs
        return jnp.where(low, prod + pltpu.roll(prod, MLA_ROPE, axis=1), 0.0)

    q = _dot(_rms(p[:, :Q_LORA], qg_ref[...]), wq_ref[...]) * scale
    kv = _dot(_rms(p[:, Q_LORA:Q_LORA + KV_LORA], kvg_ref[...]), wkv_ref[...])
    k_rope = rope(p[:, Q_LORA + KV_LORA:]).astype(BF16)
    for h in range(nh):
        o = 2 * LANES * h
        q_ref[0, :, o:o + nope] = q[:, nope * h:nope * (h + 1)].astype(BF16)
        q_ref[0, :, o + nope:o + 2 * LANES] = rope(
            q[:, nh * nope + LANES * h:nh * nope + LANES * (h + 1)]).astype(BF16)
        k_ref[0, :, o:o + nope] = kv[:, nope * h:nope * (h + 1)].astype(BF16)
        k_ref[0, :, o + nope:o + 2 * LANES] = k_rope
    v_ref[0] = kv[:, nh * nope:].astype(BF16)


def _mlaprep(p_mla, cs, qg, kvg, wq, wkv):
    bsz, s, n_mla = p_mla.shape
    tm = min(FFN_ROWS, s)
    row_spec = lambda w: pl.BlockSpec((1, tm, w), lambda b, i: (b, i, 0))
    qk_w = MLA_HEADS * 2 * LANES
    v_w = MLA_HEADS * MLA_V
    return pl.pallas_call(
        functools.partial(_mlaprep_kernel, scale=(MLA_NOPE + MLA_ROPE) ** -0.5),
        grid=(bsz, s // tm),
        in_specs=[row_spec(n_mla), row_spec(LANES), _const_spec((1, Q_LORA)), _const_spec((1, KV_LORA)),
                  _const_spec(wq.shape), _const_spec(wkv.shape)],
        out_specs=[row_spec(qk_w), row_spec(qk_w), row_spec(v_w)],
        out_shape=[jax.ShapeDtypeStruct((bsz, s, qk_w), BF16),
                   jax.ShapeDtypeStruct((bsz, s, qk_w), BF16),
                   jax.ShapeDtypeStruct((bsz, s, v_w), BF16)],
        compiler_params=_params(("parallel", "parallel")),
        name="mlaprep",
    )(p_mla, cs, qg.reshape(1, -1), kvg.reshape(1, -1), wq, wkv)


def _flash_kernel(q_ref, k_ref, v_ref, o_ref, *, blk):
    i = pl.program_id(2)
    q = q_ref[0]

    def scores(j):
        kb = k_ref[0, pl.ds(pl.multiple_of(j * blk, blk), blk), :]
        return lax.dot_general(q, kb, (((1,), (1,)), ((), ())), preferred_element_type=F32)

    def values(j):
        return v_ref[0, pl.ds(pl.multiple_of(j * blk, blk), blk), :]

    qc = lax.broadcasted_iota(jnp.int32, (blk, blk), 0) // ATTN_CHUNK
    kc = lax.broadcasted_iota(jnp.int32, (blk, blk), 1) // ATTN_CHUNK
    s = jnp.where(kc <= qc, scores(i), -1e30)
    m = jnp.max(s, axis=-1, keepdims=True)
    p = jnp.exp(s - m)
    l = jnp.sum(p, axis=-1, keepdims=True)
    acc = jnp.dot(p.astype(BF16), values(i), preferred_element_type=F32)

    def body(j, carry):
        m, l, acc = carry
        s = scores(j)
        m_new = jnp.maximum(m, jnp.max(s, axis=-1, keepdims=True))
        alpha = jnp.exp(m - m_new)
        p = jnp.exp(s - m_new)
        l = alpha * l + jnp.sum(p, axis=-1, keepdims=True)
        acc = alpha * acc + jnp.dot(p.astype(BF16), values(j), preferred_element_type=F32)
        return m_new, l, acc

    m, l, acc = lax.fori_loop(0, i, body, (m, l, acc))
    o_ref[0] = acc / l


def _flash(q, k, v):
    bsz, s, _ = q.shape
    blk = min(ATTN_BLOCK, s)
    qk_w = 2 * LANES
    return pl.pallas_call(
        functools.partial(_flash_kernel, blk=blk),
        grid=(bsz, MLA_HEADS, s // blk),
        in_specs=[pl.BlockSpec((1, blk, qk_w), lambda b, h, i: (b, i, h)),
                  pl.BlockSpec((1, s, qk_w), lambda b, h, i: (b, 0, h)),
                  pl.BlockSpec((1, s, MLA_V), lambda b, h, i: (b, 0, h))],
        out_specs=pl.BlockSpec((1, blk, MLA_V), lambda b, h, i: (b, i, h)),
        out_shape=jax.ShapeDtypeStruct((bsz, s, MLA_HEADS * MLA_V), F32),
        compiler_params=_params(("parallel", "parallel", "arbitrary")),
        name="flash",
    )(q, k, v)


def _unit_lower_inverse(a, eye, blockdiag):
    ad = jnp.where(blockdiag, a, 0.0)
    ao = a - ad
    a2 = _dot(ad, ad)
    a4 = _dot(a2, a2)
    a8 = _dot(a4, a4)
    td = eye + ad
    td = td + _dot(td, a2)
    td = td + _dot(td, a4)
    td = td + _dot(td, a8)
    n1 = _dot(td, ao)
    n2 = _dot(n1, n1)
    x = eye + n1
    x = x + _dot(x, n2)
    return _dot(x, td)


def _rwkv_kernel(p_ref, pprev_ref, mix_ref, w0_ref, a0_ref, kk_ref, ka_ref, rk_ref, lnw_ref, lnb_ref,
                 wl_ref, g2_ref, ones_ref, tri_ref, o_ref, state_ref, y_ref):
    t = pl.program_id(1)
    nl = RWKV_CHUNK
    hd = RWKV_HEAD
    c_w = o_ref.shape[-1]
    nheads = c_w // hd

    @pl.when(t == 0)
    def _():
        state_ref[...] = jnp.zeros_like(state_ref)

    p = p_ref[0]
    prev_last = jnp.where(t == 0, 0.0, pprev_ref[0][7:8, :])
    rowi = lax.broadcasted_iota(jnp.int32, p.shape, 0)
    p_prev = jnp.where(rowi == 0, prev_last, pltpu.roll(p, 1, axis=0))
    p = p + (p_prev - p) * mix_ref[...]

    r = p[:, :c_w]
    k = p[:, c_w:2 * c_w]
    v = p[:, 2 * c_w:3 * c_w]
    xwa = p[:, 3 * c_w:3 * c_w + LANES]
    xg = p[:, 3 * c_w + LANES:]

    def segsum(x):
        halves = []
        for o in range(0, c_w, 2 * LANES):
            hi, lo = _split2(x[:, o:o + 2 * LANES])
            halves.append(jnp.dot(hi, ones_ref[...], preferred_element_type=F32)
                          + jnp.dot(lo, ones_ref[...], preferred_element_type=F32))
        return jnp.concatenate(halves, axis=1)

    lane = lax.broadcasted_iota(jnp.int32, xwa.shape, 1)
    wa = _dot(jnp.where(lane < DECAY_LORA, jnp.tanh(xwa), xwa), wl_ref[...])
    z = w0_ref[...] + wa[:, :c_w]
    w_log = jnp.minimum(z, 0.0) - jnp.log1p(jnp.exp(-jnp.abs(z))) - 0.5
    lw = -jnp.exp(w_log)
    a = _sigmoid(a0_ref[...] + wa[:, c_w:])
    g = _dot(_sigmoid(xg), g2_ref[...])
    kk = k * kk_ref[...]
    kk = kk * lax.rsqrt(jnp.maximum(segsum(kk * kk), 1e-24))
    k2 = k * (1.0 + (a - 1.0) * ka_ref[...])
    bb = kk * a

    tri = tri_ref[...]
    c = sum(jnp.dot(tri, part, preferred_element_type=F32) for part in _split3(lw))
    c_last = c[nl - 1:nl, :]
    e_rem = jnp.exp(c_last - c)
    e_neg = jnp.exp(-c)
    a_t = -kk * jnp.exp(c - lw)
    r_t = r * jnp.exp(c)
    b_t = bb * e_neg
    k_t = k2 * e_neg
    b_p = bb * e_rem
    k_p = k2 * e_rem
    e_last = jnp.exp(c_last)

    ri = lax.broadcasted_iota(jnp.int32, (nl, nl), 0)
    ci = lax.broadcasted_iota(jnp.int32, (nl, nl), 1)
    strict = ci < ri
    incl = ci <= ri
    eye = (ci == ri).astype(F32)
    blockdiag = (ri // INV_BLOCK) == (ci // INV_BLOCK)

    for h in range(nheads):
        sl = slice(hd * h, hd * (h + 1))
        ar = jnp.concatenate([a_t[:, sl], r_t[:, sl]], axis=0)
        gb = _dot_nt(ar, b_t[:, sl])
        gk = _dot_nt(ar, k_t[:, sl])
        a_ab = jnp.where(strict, gb[:nl], 0.0)
        a_rb = jnp.where(incl, gb[nl:], 0.0)
        a_ak = jnp.where(strict, gk[:nl], 0.0)
        a_rk = jnp.where(incl, gk[nl:], 0.0)
        tinv = _unit_lower_inverse(a_ab, eye, blockdiag)
        v_h = v[:, sl]
        st = state_ref[h]
        sz = _dot_nt(ar, st)
        u = _dot(tinv, sz[:nl] + _dot(a_ak, v_h))
        y_ref[:, sl] = sz[nl:] + _dot(a_rb, u) + _dot(a_rk, v_h)
        state_ref[h] = st * e_last[:, sl] + _dot_tn(u, b_p[:, sl]) + _dot_tn(v_h, k_p[:, sl])

    y = y_ref[...]
    inv_n = 1.0 / hd
    mu = segsum(y) * inv_n
    d = y - mu
    var = segsum(d * d) * inv_n
    yn = d * lax.rsqrt(var + GN_EPS) * lnw_ref[...] + lnb_ref[...]
    bonus = segsum(r * k2 * rk_ref[...]) * v
    o_ref[0] = (yn + bonus) * g


def _rwkv(p_rw, mix, w0, a0, k_k, k_a, r_k, ln_w, ln_b, wl, g2):
    bsz, s, n_rw = p_rw.shape
    c_w = w0.shape[-1]
    nl = RWKV_CHUNK
    nheads = c_w // RWKV_HEAD
    seg = jnp.arange(2 * LANES) // RWKV_HEAD
    ones_bd = (seg[:, None] == seg[None, :]).astype(BF16)
    tri = (jnp.arange(nl)[None, :] <= jnp.arange(nl)[:, None]).astype(BF16)
    vec = lambda x: x.reshape(1, -1)
    sub = nl // 8
    return pl.pallas_call(
        _rwkv_kernel,
        grid=(bsz, s // nl),
        in_specs=[pl.BlockSpec((1, nl, n_rw), lambda b, t: (b, t, 0)),
                  pl.BlockSpec((1, 8, n_rw), lambda b, t: (b, jnp.maximum(t * sub - 1, 0), 0)),
                  _const_spec((1, n_rw))] + [_const_spec((1, c_w))] * 7
                 + [_const_spec(wl.shape), _const_spec(g2.shape), _const_spec(ones_bd.shape),
                    _const_spec(tri.shape)],
        out_specs=pl.BlockSpec((1, nl, c_w), lambda b, t: (b, t, 0)),
        out_shape=jax.ShapeDtypeStruct((bsz, s, c_w), F32),
        scratch_shapes=[pltpu.VMEM((nheads, RWKV_HEAD, RWKV_HEAD), F32),
                        pltpu.VMEM((nl, c_w), F32)],
        compiler_params=_params(("parallel", "arbitrary")),
        name="rwkv",
    )(p_rw, p_rw, vec(mix), vec(w0), vec(a0), vec(k_k), vec(k_a), vec(r_k), vec(ln_w), vec(ln_b),
      wl, g2, ones_bd, tri)


def _swap_halves(w):
    half = w.shape[-1] // 2
    return jnp.concatenate([w[..., half:], w[..., :half]], axis=-1)


def kernel(x, c, positions, w_mod, b_mod, ffn1_norm_g, ffn1_w_gate, ffn1_w_up, ffn1_w_down, mix_norm_g, w_in, q_norm_g, w_uq, kv_norm_g, w_ukv, attn_out_norm_g, rwkv_shift_mix, rwkv_w0, rwkv_w2, rwkv_a0, rwkv_a2, rwkv_g2, rwkv_k_k, rwkv_k_a, rwkv_r_k, rwkv_ln_w, rwkv_ln_b, w_out, ffn2_norm_g, ffn2_w_gate, ffn2_w_up, ffn2_w_down, final_norm_g):
    depth = w_mod.shape[0]
    bsz, s, d = x.shape
    nh, nope, rope_w, vw = MLA_HEADS, MLA_NOPE, MLA_ROPE, MLA_V
    mla_cols = Q_LORA + KV_LORA + rope_w
    mla_w = MLA_HEADS * MLA_V

    half = rope_w // 2
    inv_freq = ROPE_BASE ** (-jnp.arange(half, dtype=F32) / half)
    ang = positions.astype(F32)[..., None] * inv_freq
    cos, sin = jnp.cos(ang), jnp.sin(ang)
    cs = jnp.concatenate([cos, cos, -sin, sin], axis=-1)

    h = x
    for l in range(depth):
        w_kr = w_in[l][:, Q_LORA + KV_LORA:mla_cols]
        w_in_l = jnp.concatenate([w_in[l][:, :mla_cols], _swap_halves(w_kr), w_in[l][:, mla_cols:]],
                                 axis=1).astype(BF16)
        n_mla = mla_cols + rope_w
        wq = w_uq[l].reshape(Q_LORA, nh, nope + rope_w)
        wq_rope = wq[:, :, nope:]
        wq_l = jnp.concatenate(
            [wq[:, :, :nope].reshape(Q_LORA, nh * nope),
             jnp.concatenate([wq_rope, _swap_halves(wq_rope)], axis=-1).reshape(Q_LORA, nh * 2 * rope_w)],
            axis=1).astype(BF16)
        wkv = w_ukv[l].reshape(KV_LORA, nh, nope + vw)
        wkv_l = jnp.concatenate([wkv[:, :, :nope].reshape(KV_LORA, nh * nope),
                                 wkv[:, :, nope:].reshape(KV_LORA, nh * vw)], axis=1).astype(BF16)
        c_w = rwkv_w0.shape[-1]
        zeros = jnp.zeros((DECAY_LORA, c_w), F32)
        wl = jnp.concatenate([jnp.concatenate([rwkv_w2[l], zeros], axis=1),
                              jnp.concatenate([zeros, rwkv_a2[l]], axis=1)], axis=0).astype(BF16)

        mod = _mod(c, w_mod[l], b_mod[l]).reshape(bsz, N_MOD, d)
        h = _ffn(h, mod, ffn1_norm_g[l], ffn1_w_gate[l].astype(BF16), ffn1_w_up[l].astype(BF16),
                 ffn1_w_down[l].astype(BF16), mod_rows=(0, 1, 2))
        p_mla, p_rw = _inproj(h, mod, mix_norm_g[l], w_in_l, n_mla)
        q, k, v = _mlaprep(p_mla, cs, q_norm_g[l], kv_norm_g[l], wq_l, wkv_l)
        y_a = _flash(q, k, v)
        y_b = _rwkv(p_rw, rwkv_shift_mix[l], rwkv_w0[l], rwkv_a0[l], rwkv_k_k[l], rwkv_k_a[l],
                    rwkv_r_k[l].reshape(-1), rwkv_ln_w[l], rwkv_ln_b[l], wl, rwkv_g2[l].astype(BF16))
        mix = (y_a, y_b, attn_out_norm_g[l], w_out[l][:mla_w].astype(BF16), w_out[l][mla_w:].astype(BF16))
        h = _ffn(h, mod, ffn2_norm_g[l], ffn2_w_gate[l].astype(BF16), ffn2_w_up[l].astype(BF16),
                 ffn2_w_down[l].astype(BF16), mod_rows=(6, 7, 8, 5), mix=mix,
                 final_g=final_norm_g if l == depth - 1 else None)
    return h
```

```python
import functools

import jax
import jax.numpy as jnp
from jax import lax
from jax.experimental import pallas as pl
from jax.experimental.pallas import tpu as pltpu

F32 = jnp.float32
BF16 = jnp.bfloat16

MLA_HEADS = 4
MLA_NOPE = 128
MLA_ROPE = 64
MLA_V = 128
Q_LORA = 384
KV_LORA = 256
ROPE_BASE = 10000.0
RWKV_HEAD = 64
DECAY_LORA = 64
AAA_LORA = 64
GATE_LORA = 128
NORM_EPS = 1e-6
GN_EPS = 64e-5
N_MOD = 9
ATTN_CHUNK = 64
LOG2_E = 1.4426950408889634

LANES = 128
VMEM_LIMIT_BYTES = 56 * 1024 * 1024

FFN_ROWS = 512
ATTN_CHAIN_ROWS = 512
RWKV_CHUNK = 64
RWKV_CHUNKS_PER_STEP = 2
INV_BLOCK = 16


def _dot(a, b):
    return jnp.dot(a.astype(BF16), b.astype(BF16), preferred_element_type=F32)


def _batched(a, b, ca, cb):
    return lax.dot_general(a.astype(BF16), b.astype(BF16), (((ca,), (cb,)), ((0,), (0,))),
                           preferred_element_type=F32)


def _bdot(a, b):
    return _batched(a, b, 2, 1)


def _bdot_nt(a, b):
    return _batched(a, b, 2, 2)


def _bdot_tn(a, b):
    return _batched(a, b, 1, 1)


def _split2(x):
    hi = x.astype(BF16)
    lo = (x - hi.astype(F32)).astype(BF16)
    return hi, lo


def _split3(x):
    hi = x.astype(BF16)
    r1 = x - hi.astype(F32)
    mid = r1.astype(BF16)
    lo = (r1 - mid.astype(F32)).astype(BF16)
    return hi, mid, lo


def _sigmoid(x):
    return 1.0 / (1.0 + jnp.exp(-x))


def _rms(x, g):
    return x * lax.rsqrt(jnp.mean(x * x, axis=-1, keepdims=True) + NORM_EPS) * g


def _const_spec(shape):
    nd = len(shape)
    return pl.BlockSpec(shape, lambda *_: (0,) * nd, pipeline_mode=pl.Buffered(1))


def _params(semantics):
    return pltpu.CompilerParams(dimension_semantics=semantics, vmem_limit_bytes=VMEM_LIMIT_BYTES)


def _mod_kernel(c_ref, w_ref, b_ref, o_ref):
    c = c_ref[...]
    c_act = c * _sigmoid(c)
    o_ref[...] = _dot(c_act, w_ref[...]) + b_ref[...]


def _mod(c, w_mod, b_mod):
    bsz, d = c.shape
    n = w_mod.shape[1]
    tn = d
    return pl.pallas_call(
        _mod_kernel,
        grid=(n // tn,),
        in_specs=[pl.BlockSpec((bsz, d), lambda j: (0, 0)),
                  pl.BlockSpec((d, tn), lambda j: (0, j)),
                  pl.BlockSpec((1, tn), lambda j: (0, j))],
        out_specs=pl.BlockSpec((bsz, tn), lambda j: (0, j)),
        out_shape=jax.ShapeDtypeStruct((bsz, n), F32),
        compiler_params=_params(("arbitrary",)),
        name="mod",
    )(c, w_mod, b_mod.reshape(1, n))


def _ffn_kernel(*refs, mod_rows, pre_mix, final_norm):
    it = iter(refs)
    h_ref, mod_ref, ng_ref, wg_ref, wu_ref, wd_ref = (next(it) for _ in range(6))
    if pre_mix:
        ya_ref, yb_ref, ang_ref, woa_ref, wob_ref = (next(it) for _ in range(5))
    if final_norm:
        fng_ref = next(it)
    o_ref = next(it)

    h = h_ref[0]
    mod = mod_ref[0]
    row = lambda i: mod[i:i + 1, :]
    if pre_mix:
        ya = _rms(ya_ref[0], ang_ref[...])
        y = _dot(ya, woa_ref[...]) + _dot(yb_ref[0], wob_ref[...])
        h = h + row(mod_rows[3]) * y
    sh, sc, gt = row(mod_rows[0]), row(mod_rows[1]), row(mod_rows[2])
    u = (_rms(h, ng_ref[...]) * (1.0 + sc) + sh).astype(BF16)
    g = jnp.dot(u, wg_ref[...], preferred_element_type=F32)
    up = jnp.dot(u, wu_ref[...], preferred_element_type=F32)
    act = (g * _sigmoid(g) * up).astype(BF16)
    f = jnp.dot(act, wd_ref[...], preferred_element_type=F32)
    h = h + 0.5 * gt * f
    if final_norm:
        h = _rms(h, fng_ref[...])
    o_ref[0] = h


def _ffn(h, mod, ng, wg, wu, wd, *, mod_rows, mix=None, final_g=None):
    bsz, s, d = h.shape
    f = wg.shape[1]
    tm = min(FFN_ROWS, s)
    row_spec = lambda w: pl.BlockSpec((1, tm, w), lambda b, i: (b, i, 0))
    in_specs = [row_spec(d),
                pl.BlockSpec((1, N_MOD, d), lambda b, i: (b, 0, 0)),
                _const_spec((1, d)), _const_spec((d, f)), _const_spec((d, f)), _const_spec((f, d))]
    args = [h, mod, ng.reshape(1, d), wg, wu, wd]
    if mix is not None:
        ya, yb, ang, woa, wob = mix
        wa, wb = ya.shape[-1], yb.shape[-1]
        in_specs += [row_spec(wa), row_spec(wb), _const_spec((1, wa)),
                     _const_spec((wa, d)), _const_spec((wb, d))]
        args += [ya, yb, ang.reshape(1, wa), woa, wob]
    if final_g is not None:
        in_specs.append(_const_spec((1, d)))
        args.append(final_g.reshape(1, d))
    kern = functools.partial(_ffn_kernel, mod_rows=mod_rows, pre_mix=mix is not None,
                             final_norm=final_g is not None)
    return pl.pallas_call(
        kern,
        grid=(bsz, s // tm),
        in_specs=in_specs,
        out_specs=row_spec(d),
        out_shape=jax.ShapeDtypeStruct((bsz, s, d), F32),
        compiler_params=_params(("parallel", "parallel")),
        name="ffn_mix" if mix is not None else "ffn",
    )(*args)


def _inproj_kernel(h_ref, mod_ref, ng_ref, w_ref, pm_ref, pr_ref, *, n_mla):
    mod = mod_ref[0]
    u = _rms(h_ref[0], ng_ref[...]) * (1.0 + mod[4:5, :]) + mod[3:4, :]
    p = _dot(u, w_ref[...])
    pm_ref[0] = p[:, :n_mla]
    pr_ref[0] = p[:, n_mla:]


def _inproj(h, mod, ng, w, n_mla):
    bsz, s, d = h.shape
    n = w.shape[1]
    tm = min(FFN_ROWS, s)
    row_spec = lambda wd_: pl.BlockSpec((1, tm, wd_), lambda b, i: (b, i, 0))
    return pl.pallas_call(
        functools.partial(_inproj_kernel, n_mla=n_mla),
        grid=(bsz, s // tm),
        in_specs=[row_spec(d), pl.BlockSpec((1, N_MOD, d), lambda b, i: (b, 0, 0)),
                  _const_spec((1, d)), _const_spec((d, n))],
        out_specs=[row_spec(n_mla), row_spec(n - n_mla)],
        out_shape=[jax.ShapeDtypeStruct((bsz, s, n_mla), F32),
                   jax.ShapeDtypeStruct((bsz, s, n - n_mla), F32)],
        compiler_params=_params(("parallel", "parallel")),
        name="inproj",
    )(h, mod, ng.reshape(1, d), w)


def _mlaprep_kernel(p_ref, cs_ref, qg_ref, kvg_ref, wq_ref, wkv_ref, q_ref, k_ref, v_ref, *, scale):
    p = p_ref[0]
    cs = cs_ref[0]
    nh, nope = MLA_HEADS, MLA_NOPE
    lane = lax.broadcasted_iota(jnp.int32, cs.shape, 1)
    low = lane < MLA_ROPE

    def rope(xpair):
        prod = xpair * cs
        return jnp.where(low, prod + pltpu.roll(prod, MLA_ROPE, axis=1), 0.0)

    q = _dot(_rms(p[:, :Q_LORA], qg_ref[...]), wq_ref[...]) * scale
    kv = _dot(_rms(p[:, Q_LORA:Q_LORA + KV_LORA], kvg_ref[...]), wkv_ref[...])
    k_rope = rope(p[:, Q_LORA + KV_LORA:]).astype(BF16)
    for h in range(nh):
        o = 2 * LANES * h
        q_ref[0, :, o:o + nope] = q[:, nope * h:nope * (h + 1)].astype(BF16)
        q_ref[0, :, o + nope:o + 2 * LANES] = rope(
            q[:, nh * nope + LANES * h:nh * nope + LANES * (h + 1)]).astype(BF16)
        k_ref[0, :, o:o + nope] = kv[:, nope * h:nope * (h + 1)].astype(BF16)
        k_ref[0, :, o + nope:o + 2 * LANES] = k_rope
    v_ref[0] = kv[:, nh * nope:].astype(BF16)


def _mlaprep(p_mla, cs, qg, kvg, wq, wkv):
    bsz, s, n_mla = p_mla.shape
    tm = min(FFN_ROWS, s)
    row_spec = lambda w: pl.BlockSpec((1, tm, w), lambda b, i: (b, i, 0))
    qk_w = MLA_HEADS * 2 * LANES
    v_w = MLA_HEADS * MLA_V
    return pl.pallas_call(
        functools.partial(_mlaprep_kernel, scale=(MLA_NOPE + MLA_ROPE) ** -0.5 * LOG2_E),
        grid=(bsz, s // tm),
        in_specs=[row_spec(n_mla), row_spec(LANES), _const_spec((1, Q_LORA)), _const_spec((1, KV_LORA)),
                  _const_spec(wq.shape), _const_spec(wkv.shape)],
        out_specs=[row_spec(qk_w), row_spec(qk_w), row_spec(v_w)],
        out_shape=[jax.ShapeDtypeStruct((bsz, s, qk_w), BF16),
                   jax.ShapeDtypeStruct((bsz, s, qk_w), BF16),
                   jax.ShapeDtypeStruct((bsz, s, v_w), BF16)],
        compiler_params=_params(("parallel", "parallel")),
        name="mlaprep",
    )(p_mla, cs, qg.reshape(1, -1), kvg.reshape(1, -1), wq, wkv)


def _flash_kernel(q_ref, k_ref, v_ref, o_ref, *, bq):
    i = pl.program_id(2)
    bk = 2 * bq
    base = pl.multiple_of(i * bk, bk)

    def scores(chain, start, n):
        kb = k_ref[0, pl.ds(start, n), :]
        return lax.dot_general(q_ref[0, chain * bq:(chain + 1) * bq, :], kb, (((1,), (1,)), ((), ())),
                               preferred_element_type=F32)

    def first(s, start, n):
        m = jnp.max(s, axis=-1, keepdims=True)
        p = jnp.exp2(s - m)
        l = jnp.sum(p, axis=-1, keepdims=True)
        return m, l, jnp.dot(p.astype(BF16), v_ref[0, pl.ds(start, n), :], preferred_element_type=F32)

    def update(carry, s, start, n):
        m, l, acc = carry
        m_new = jnp.maximum(m, jnp.max(s, axis=-1, keepdims=True))
        alpha = jnp.exp2(m - m_new)
        p = jnp.exp2(s - m_new)
        l = alpha * l + jnp.sum(p, axis=-1, keepdims=True)
        pv = jnp.dot(p.astype(BF16), v_ref[0, pl.ds(start, n), :], preferred_element_type=F32)
        return m_new, l, alpha * acc + pv

    visible = (lax.broadcasted_iota(jnp.int32, (bq, bq), 1) // ATTN_CHUNK
               <= lax.broadcasted_iota(jnp.int32, (bq, bq), 0) // ATTN_CHUNK)
    mid = pl.multiple_of(base + bq, bq)
    s_a = jnp.where(visible, scores(0, base, bq), -1e30)
    s_b = jnp.where(visible, scores(1, mid, bq), -1e30)
    s_b0 = scores(1, base, bq)
    chain_a = first(s_a, base, bq)
    chain_b = update(first(s_b, mid, bq), s_b0, base, bq)

    def body(j, carry):
        chain_a, chain_b = carry
        start = pl.multiple_of(j * bk, bk)
        s_a = scores(0, start, bk)
        s_b = scores(1, start, bk)
        return update(chain_a, s_a, start, bk), update(chain_b, s_b, start, bk)

    chain_a, chain_b = lax.fori_loop(0, i, body, (chain_a, chain_b))
    o_ref[0, :bq, :] = chain_a[2] / chain_a[1]
    o_ref[0, bq:, :] = chain_b[2] / chain_b[1]


def _flash(q, k, v):
    bsz, s, _ = q.shape
    bq = min(ATTN_CHAIN_ROWS, s // 2)
    qk_w = 2 * LANES
    return pl.pallas_call(
        functools.partial(_flash_kernel, bq=bq),
        grid=(bsz, MLA_HEADS, s // (2 * bq)),
        in_specs=[pl.BlockSpec((1, 2 * bq, qk_w), lambda b, h, i: (b, i, h)),
                  pl.BlockSpec((1, s, qk_w), lambda b, h, i: (b, 0, h)),
                  pl.BlockSpec((1, s, MLA_V), lambda b, h, i: (b, 0, h))],
        out_specs=pl.BlockSpec((1, 2 * bq, MLA_V), lambda b, h, i: (b, i, h)),
        out_shape=jax.ShapeDtypeStruct((bsz, s, MLA_HEADS * MLA_V), F32),
        compiler_params=_params(("parallel", "parallel", "arbitrary")),
        name="flash",
    )(q, k, v)


def _unit_lower_inverse(a, eye, blockdiag):
    ad = jnp.where(blockdiag, a, 0.0)
    ao = a - ad
    a2 = _bdot(ad, ad)
    a4 = _bdot(a2, a2)
    a8 = _bdot(a4, a4)
    td = eye + ad
    td = td + _bdot(td, a2)
    td = td + _bdot(td, a4)
    td = td + _bdot(td, a8)
    n1 = _bdot(td, ao)
    n2 = _bdot(n1, n1)
    x = eye + n1
    x = x + _bdot(x, n2)
    return _bdot(x, td)


def _rwkv_kernel(p_ref, pprev_ref, mix_ref, w0_ref, a0_ref, kk_ref, ka_ref, rk_ref, lnw_ref, lnb_ref,
                 wl_ref, g2_ref, ones_ref, tri_ref, o_ref, state_ref, y_ref, *, nchunks):
    t = pl.program_id(1)
    nl = RWKV_CHUNK
    hd = RWKV_HEAD
    c_w = o_ref.shape[-1]
    nheads = c_w // hd

    @pl.when(t == 0)
    def _():
        state_ref[...] = jnp.zeros_like(state_ref)

    p = p_ref[0]
    prev_last = jnp.where(t == 0, 0.0, pprev_ref[0][7:8, :])
    rowi = lax.broadcasted_iota(jnp.int32, p.shape, 0)
    p_prev = jnp.where(rowi == 0, prev_last, pltpu.roll(p, 1, axis=0))
    p = p + (p_prev - p) * mix_ref[...]

    r = p[:, :c_w]
    k = p[:, c_w:2 * c_w]
    v = p[:, 2 * c_w:3 * c_w]
    xwa = p[:, 3 * c_w:3 * c_w + LANES]
    xg = p[:, 3 * c_w + LANES:]

    def segsum(x):
        halves = []
        for o in range(0, c_w, 2 * LANES):
            hi, lo = _split2(x[:, o:o + 2 * LANES])
            halves.append(jnp.dot(hi, ones_ref[...], preferred_element_type=F32)
                          + jnp.dot(lo, ones_ref[...], preferred_element_type=F32))
        return jnp.concatenate(halves, axis=1)

    lane = lax.broadcasted_iota(jnp.int32, xwa.shape, 1)
    wa = _dot(jnp.where(lane < DECAY_LORA, jnp.tanh(xwa), xwa), wl_ref[...])
    z = w0_ref[...] + wa[:, :c_w]
    w_log = jnp.minimum(z, 0.0) - jnp.log1p(jnp.exp(-jnp.abs(z))) - 0.5
    lw = -jnp.exp(w_log)
    a = _sigmoid(a0_ref[...] + wa[:, c_w:])
    g = _dot(_sigmoid(xg), g2_ref[...])
    kk = k * kk_ref[...]
    kk = kk * lax.rsqrt(jnp.maximum(segsum(kk * kk), 1e-24))
    k2 = k * (1.0 + (a - 1.0) * ka_ref[...])
    bb = kk * a

    tri = tri_ref[...]
    c = sum(jnp.dot(tri, part, preferred_element_type=F32) for part in _split3(lw))
    rows = lambda x, j: x[nl * j:nl * (j + 1)]
    c_last = [rows(c, j)[nl - 1:nl, :] for j in range(nchunks)]
    e_rem = jnp.exp(jnp.concatenate([c_last[j] - rows(c, j) for j in range(nchunks)], axis=0))
    e_neg = jnp.exp(-c)
    a_t = -kk * jnp.exp(c - lw)
    r_t = r * jnp.exp(c)
    b_t = bb * e_neg
    k_t = k2 * e_neg
    b_p = bb * e_rem
    k_p = k2 * e_rem

    def heads(x, j):
        xj = rows(x, j) if x.shape[0] > 1 else x
        return jnp.stack([xj[:, hd * h:hd * (h + 1)] for h in range(nheads)], axis=0)

    def all_heads(x):
        return jnp.concatenate([heads(x, j) for j in range(nchunks)], axis=0)

    ri = lax.broadcasted_iota(jnp.int32, (1, nl, nl), 1)
    ci = lax.broadcasted_iota(jnp.int32, (1, nl, nl), 2)
    strict = ci < ri
    incl = ci <= ri
    eye = (ci == ri).astype(F32)
    blockdiag = (ri // INV_BLOCK) == (ci // INV_BLOCK)

    ar = jnp.concatenate([all_heads(a_t), all_heads(r_t)], axis=1)
    v_h = all_heads(v)
    gb = _bdot_nt(ar, all_heads(b_t))
    gk = _bdot_nt(ar, all_heads(k_t))
    a_rb = jnp.where(incl, gb[:, nl:], 0.0)
    a_ak = jnp.where(strict, gk[:, :nl], 0.0)
    a_rk = jnp.where(incl, gk[:, nl:], 0.0)
    tinv = _unit_lower_inverse(jnp.where(strict, gb[:, :nl], 0.0), eye, blockdiag)
    akv = _bdot(a_ak, v_h)
    rkv = _bdot(a_rk, v_h)
    kpv = _bdot_tn(v_h, all_heads(k_p))
    b_ph = all_heads(b_p)

    st = state_ref[...]
    for j in range(nchunks):
        sj = slice(nheads * j, nheads * (j + 1))
        sz = _bdot_nt(ar[sj], st)
        u = _bdot(tinv[sj], sz[:, :nl] + akv[sj])
        y_j = sz[:, nl:] + _bdot(a_rb[sj], u) + rkv[sj]
        for h in range(nheads):
            y_ref[nl * j:nl * (j + 1), hd * h:hd * (h + 1)] = y_j[h]
        st = st * heads(jnp.exp(c_last[j]), 0) + _bdot_tn(u, b_ph[sj]) + kpv[sj]
    state_ref[...] = st

    y = y_ref[...]
    inv_n = 1.0 / hd
    mu = segsum(y) * inv_n
    d = y - mu
    var = segsum(d * d) * inv_n
    yn = d * lax.rsqrt(var + GN_EPS) * lnw_ref[...] + lnb_ref[...]
    bonus = segsum(r * k2 * rk_ref[...]) * v
    o_ref[0] = (yn + bonus) * g


def _rwkv(p_rw, mix, w0, a0, k_k, k_a, r_k, ln_w, ln_b, wl, g2):
    bsz, s, n_rw = p_rw.shape
    c_w = w0.shape[-1]
    nchunks = min(RWKV_CHUNKS_PER_STEP, s // RWKV_CHUNK)
    nl = RWKV_CHUNK * nchunks
    nheads = c_w // RWKV_HEAD
    seg = jnp.arange(2 * LANES) // RWKV_HEAD
    ones_bd = (seg[:, None] == seg[None, :]).astype(BF16)
    pos = jnp.arange(nl)
    tri = ((pos[None, :] <= pos[:, None])
           & (pos[None, :] // RWKV_CHUNK == pos[:, None] // RWKV_CHUNK)).astype(BF16)
    vec = lambda x: x.reshape(1, -1)
    sub = nl // 8
    return pl.pallas_call(
        functools.partial(_rwkv_kernel, nchunks=nchunks),
        grid=(bsz, s // nl),
        in_specs=[pl.BlockSpec((1, nl, n_rw), lambda b, t: (b, t, 0)),
                  pl.BlockSpec((1, 8, n_rw), lambda b, t: (b, jnp.maximum(t * sub - 1, 0), 0)),
                  _const_spec((1, n_rw))] + [_const_spec((1, c_w))] * 7
                 + [_const_spec(wl.shape), _const_spec(g2.shape), _const_spec(ones_bd.shape),
                    _const_spec(tri.shape)],
        out_specs=pl.BlockSpec((1, nl, c_w), lambda b, t: (b, t, 0)),
        out_shape=jax.ShapeDtypeStruct((bsz, s, c_w), F32),
        scratch_shapes=[pltpu.VMEM((nheads, RWKV_HEAD, RWKV_HEAD), F32),
                        pltpu.VMEM((nl, c_w), F32)],
        compiler_params=_params(("parallel", "arbitrary")),
        name="rwkv",
    )(p_rw, p_rw, vec(mix), vec(w0), vec(a0), vec(k_k), vec(k_a), vec(r_k), vec(ln_w), vec(ln_b),
      wl, g2, ones_bd, tri)


def _swap_halves(w):
    half = w.shape[-1] // 2
    return jnp.concatenate([w[..., half:], w[..., :half]], axis=-1)


def kernel(x, c, positions, w_mod, b_mod, ffn1_norm_g, ffn1_w_gate, ffn1_w_up, ffn1_w_down, mix_norm_g, w_in, q_norm_g, w_uq, kv_norm_g, w_ukv, attn_out_norm_g, rwkv_shift_mix, rwkv_w0, rwkv_w2, rwkv_a0, rwkv_a2, rwkv_g2, rwkv_k_k, rwkv_k_a, rwkv_r_k, rwkv_ln_w, rwkv_ln_b, w_out, ffn2_norm_g, ffn2_w_gate, ffn2_w_up, ffn2_w_down, final_norm_g):
    depth = w_mod.shape[0]
    bsz, s, d = x.shape
    nh, nope, rope_w, vw = MLA_HEADS, MLA_NOPE, MLA_ROPE, MLA_V
    mla_cols = Q_LORA + KV_LORA + rope_w
    mla_w = MLA_HEADS * MLA_V

    half = rope_w // 2
    inv_freq = ROPE_BASE ** (-jnp.arange(half, dtype=F32) / half)
    ang = positions.astype(F32)[..., None] * inv_freq
    cos, sin = jnp.cos(ang), jnp.sin(ang)
    cs = jnp.concatenate([cos, cos, -sin, sin], axis=-1)

    h = x
    for l in range(depth):
        w_kr = w_in[l][:, Q_LORA + KV_LORA:mla_cols]
        w_in_l = jnp.concatenate([w_in[l][:, :mla_cols], _swap_halves(w_kr), w_in[l][:, mla_cols:]],
                                 axis=1).astype(BF16)
        n_mla = mla_cols + rope_w
        wq = w_uq[l].reshape(Q_LORA, nh, nope + rope_w)
        wq_rope = wq[:, :, nope:]
        wq_l = jnp.concatenate(
            [wq[:, :, :nope].reshape(Q_LORA, nh * nope),
             jnp.concatenate([wq_rope, _swap_halves(wq_rope)], axis=-1).reshape(Q_LORA, nh * 2 * rope_w)],
            axis=1).astype(BF16)
        wkv = w_ukv[l].reshape(KV_LORA, nh, nope + vw)
        wkv_l = jnp.concatenate([wkv[:, :, :nope].reshape(KV_LORA, nh * nope),
                                 wkv[:, :, nope:].reshape(KV_LORA, nh * vw)], axis=1).astype(BF16)
        c_w = rwkv_w0.shape[-1]
        zeros = jnp.zeros((DECAY_LORA, c_w), F32)
        wl = jnp.concatenate([jnp.concatenate([rwkv_w2[l], zeros], axis=1),
                              jnp.concatenate([zeros, rwkv_a2[l]], axis=1)], axis=0).astype(BF16)

        mod = _mod(c, w_mod[l], b_mod[l]).reshape(bsz, N_MOD, d)
        h = _ffn(h, mod, ffn1_norm_g[l], ffn1_w_gate[l].astype(BF16), ffn1_w_up[l].astype(BF16),
                 ffn1_w_down[l].astype(BF16), mod_rows=(0, 1, 2))
        p_mla, p_rw = _inproj(h, mod, mix_norm_g[l], w_in_l, n_mla)
        q, k, v = _mlaprep(p_mla, cs, q_norm_g[l], kv_norm_g[l], wq_l, wkv_l)
        y_a = _flash(q, k, v)
        y_b = _rwkv(p_rw, rwkv_shift_mix[l], rwkv_w0[l], rwkv_a0[l], rwkv_k_k[l], rwkv_k_a[l],
                    rwkv_r_k[l].reshape(-1), rwkv_ln_w[l], rwkv_ln_b[l], wl, rwkv_g2[l].astype(BF16))
        mix = (y_a, y_b, attn_out_norm_g[l], w_out[l][:mla_w].astype(BF16), w_out[l][mla_w:].astype(BF16))
        h = _ffn(h, mod, ffn2_norm_g[l], ffn2_w_gate[l].astype(BF16), ffn2_w_up[l].astype(BF16),
                 ffn2_w_down[l].astype(BF16), mod_rows=(6, 7, 8, 5), mix=mix,
                 final_g=final_norm_g if l == depth - 1 else None)
    return h
```

```python
import functools

import jax
import jax.numpy as jnp
from jax import lax
from jax.experimental import pallas as pl
from jax.experimental.pallas import tpu as pltpu

F32 = jnp.float32
BF16 = jnp.bfloat16

MLA_HEADS = 4
MLA_NOPE = 128
MLA_ROPE = 64
MLA_V = 128
Q_LORA = 384
KV_LORA = 256
ROPE_BASE = 10000.0
RWKV_HEAD = 64
DECAY_LORA = 64
AAA_LORA = 64
GATE_LORA = 128
NORM_EPS = 1e-6
GN_EPS = 64e-5
N_MOD = 9
ATTN_CHUNK = 64
LOG2_E = 1.4426950408889634
EXP_NEG_HALF = 0.6065306597126334

LANES = 128
VMEM_LIMIT_BYTES = 56 * 1024 * 1024

FFN_ROWS = 512
ATTN_CHAIN_ROWS = 512
RWKV_CHUNK = 64
RWKV_CHUNKS_PER_STEP = 8
RWKV_CUMSUM_ROWS = 256
INV_BLOCK = 16


def _dot(a, b):
    return jnp.dot(a.astype(BF16), b.astype(BF16), preferred_element_type=F32)


def _batched(a, b, ca, cb):
    return lax.dot_general(a.astype(BF16), b.astype(BF16), (((ca,), (cb,)), ((0,), (0,))),
                           preferred_element_type=F32)


def _bdot(a, b):
    return _batched(a, b, 2, 1)


def _bdot_nt(a, b):
    return _batched(a, b, 2, 2)


def _bdot_tn(a, b):
    return _batched(a, b, 1, 1)


def _split2(x):
    hi = x.astype(BF16)
    lo = (x - hi.astype(F32)).astype(BF16)
    return hi, lo


def _split3(x):
    hi = x.astype(BF16)
    r1 = x - hi.astype(F32)
    mid = r1.astype(BF16)
    lo = (r1 - mid.astype(F32)).astype(BF16)
    return hi, mid, lo


def _sigmoid(x):
    return 1.0 / (1.0 + jnp.exp(-x))


def _rms(x, g):
    return x * lax.rsqrt(jnp.mean(x * x, axis=-1, keepdims=True) + NORM_EPS) * g


def _const_spec(shape):
    nd = len(shape)
    return pl.BlockSpec(shape, lambda *_: (0,) * nd, pipeline_mode=pl.Buffered(1))


def _params(semantics):
    return pltpu.CompilerParams(dimension_semantics=semantics, vmem_limit_bytes=VMEM_LIMIT_BYTES)


def _mod_kernel(c_ref, w_ref, b_ref, o_ref):
    c = c_ref[...]
    c_act = c * _sigmoid(c)
    o_ref[...] = _dot(c_act, w_ref[...]) + b_ref[...]


def _mod(c, w_mod, b_mod):
    bsz, d = c.shape
    n = w_mod.shape[1]
    tn = d
    return pl.pallas_call(
        _mod_kernel,
        grid=(n // tn,),
        in_specs=[pl.BlockSpec((bsz, d), lambda j: (0, 0)),
                  pl.BlockSpec((d, tn), lambda j: (0, j)),
                  pl.BlockSpec((1, tn), lambda j: (0, j))],
        out_specs=pl.BlockSpec((bsz, tn), lambda j: (0, j)),
        out_shape=jax.ShapeDtypeStruct((bsz, n), F32),
        compiler_params=_params(("arbitrary",)),
        name="mod",
    )(c, w_mod, b_mod.reshape(1, n))


def _ffn_kernel(*refs, mod_rows, pre_mix, final_norm):
    it = iter(refs)
    h_ref, mod_ref, ng_ref, wg_ref, wu_ref, wd_ref = (next(it) for _ in range(6))
    if pre_mix:
        ya_ref, yb_ref, ang_ref, woa_ref, wob_ref = (next(it) for _ in range(5))
    if final_norm:
        fng_ref = next(it)
    o_ref = next(it)

    h = h_ref[0]
    mod = mod_ref[0]
    row = lambda i: mod[i:i + 1, :]
    if pre_mix:
        ya = _rms(ya_ref[0], ang_ref[...])
        y = _dot(ya, woa_ref[...]) + _dot(yb_ref[0], wob_ref[...])
        h = h + row(mod_rows[3]) * y
    sh, sc, gt = row(mod_rows[0]), row(mod_rows[1]), row(mod_rows[2])
    u = (_rms(h, ng_ref[...]) * (1.0 + sc) + sh).astype(BF16)
    g = jnp.dot(u, wg_ref[...], preferred_element_type=F32)
    up = jnp.dot(u, wu_ref[...], preferred_element_type=F32)
    act = (g * _sigmoid(g) * up).astype(BF16)
    f = jnp.dot(act, wd_ref[...], preferred_element_type=F32)
    h = h + 0.5 * gt * f
    if final_norm:
        h = _rms(h, fng_ref[...])
    o_ref[0] = h


def _ffn(h, mod, ng, wg, wu, wd, *, mod_rows, mix=None, final_g=None):
    bsz, s, d = h.shape
    f = wg.shape[1]
    tm = min(FFN_ROWS, s)
    row_spec = lambda w: pl.BlockSpec((1, tm, w), lambda b, i: (b, i, 0))
    in_specs = [row_spec(d),
                pl.BlockSpec((1, N_MOD, d), lambda b, i: (b, 0, 0)),
                _const_spec((1, d)), _const_spec((d, f)), _const_spec((d, f)), _const_spec((f, d))]
    args = [h, mod, ng.reshape(1, d), wg, wu, wd]
    if mix is not None:
        ya, yb, ang, woa, wob = mix
        wa, wb = ya.shape[-1], yb.shape[-1]
        in_specs += [row_spec(wa), row_spec(wb), _const_spec((1, wa)),
                     _const_spec((wa, d)), _const_spec((wb, d))]
        args += [ya, yb, ang.reshape(1, wa), woa, wob]
    if final_g is not None:
        in_specs.append(_const_spec((1, d)))
        args.append(final_g.reshape(1, d))
    kern = functools.partial(_ffn_kernel, mod_rows=mod_rows, pre_mix=mix is not None,
                             final_norm=final_g is not None)
    return pl.pallas_call(
        kern,
        grid=(bsz, s // tm),
        in_specs=in_specs,
        out_specs=row_spec(d),
        out_shape=jax.ShapeDtypeStruct((bsz, s, d), F32),
        compiler_params=_params(("parallel", "parallel")),
        name="ffn_mix" if mix is not None else "ffn",
    )(*args)


def _inproj_kernel(h_ref, mod_ref, ng_ref, w_ref, cs_ref, qg_ref, kvg_ref, wq_ref, wkv_ref,
                   pr_ref, q_ref, k_ref, v_ref, *, n_mla, scale):
    mod = mod_ref[0]
    u = _rms(h_ref[0], ng_ref[...]) * (1.0 + mod[4:5, :]) + mod[3:4, :]
    proj = _dot(u, w_ref[...])
    pr_ref[0] = proj[:, n_mla:]
    p = proj[:, :n_mla]
    cs = cs_ref[0]
    nh, nope = MLA_HEADS, MLA_NOPE
    lane = lax.broadcasted_iota(jnp.int32, cs.shape, 1)
    low = lane < MLA_ROPE

    def rope(xpair):
        prod = xpair * cs
        return jnp.where(low, prod + pltpu.roll(prod, MLA_ROPE, axis=1), 0.0)

    q = _dot(_rms(p[:, :Q_LORA], qg_ref[...]), wq_ref[...]) * scale
    kv = _dot(_rms(p[:, Q_LORA:Q_LORA + KV_LORA], kvg_ref[...]), wkv_ref[...])
    k_rope = rope(p[:, Q_LORA + KV_LORA:]).astype(BF16)
    for h in range(nh):
        o = 2 * LANES * h
        q_ref[0, :, o:o + nope] = q[:, nope * h:nope * (h + 1)].astype(BF16)
        q_ref[0, :, o + nope:o + 2 * LANES] = rope(
            q[:, nh * nope + LANES * h:nh * nope + LANES * (h + 1)]).astype(BF16)
        k_ref[0, :, o:o + nope] = kv[:, nope * h:nope * (h + 1)].astype(BF16)
        k_ref[0, :, o + nope:o + 2 * LANES] = k_rope
    v_ref[0] = kv[:, nh * nope:].astype(BF16)


def _inproj(h, mod, ng, w, n_mla, cs, qg, kvg, wq, wkv):
    bsz, s, d = h.shape
    n = w.shape[1]
    tm = min(FFN_ROWS, s)
    row_spec = lambda wd_: pl.BlockSpec((1, tm, wd_), lambda b, i: (b, i, 0))
    qk_w = MLA_HEADS * 2 * LANES
    v_w = MLA_HEADS * MLA_V
    return pl.pallas_call(
        functools.partial(_inproj_kernel, n_mla=n_mla, scale=(MLA_NOPE + MLA_ROPE) ** -0.5 * LOG2_E),
        grid=(bsz, s // tm),
        in_specs=[row_spec(d), pl.BlockSpec((1, N_MOD, d), lambda b, i: (b, 0, 0)),
                  _const_spec((1, d)), _const_spec((d, n)), row_spec(LANES),
                  _const_spec((1, Q_LORA)), _const_spec((1, KV_LORA)),
                  _const_spec(wq.shape), _const_spec(wkv.shape)],
        out_specs=[row_spec(n - n_mla), row_spec(qk_w), row_spec(qk_w), row_spec(v_w)],
        out_shape=[jax.ShapeDtypeStruct((bsz, s, n - n_mla), F32),
                   jax.ShapeDtypeStruct((bsz, s, qk_w), BF16),
                   jax.ShapeDtypeStruct((bsz, s, qk_w), BF16),
                   jax.ShapeDtypeStruct((bsz, s, v_w), BF16)],
        compiler_params=_params(("parallel", "parallel")),
        name="inproj",
    )(h, mod, ng.reshape(1, d), w, cs, qg.reshape(1, -1), kvg.reshape(1, -1), wq, wkv)


def _flash_kernel(q_ref, k_ref, v_ref, o_ref, *, bq):
    i = pl.program_id(2)
    bk = 2 * bq
    base = pl.multiple_of(i * bk, bk)

    def scores(chain, start, n):
        kb = k_ref[0, pl.ds(start, n), :]
        return lax.dot_general(q_ref[0, chain * bq:(chain + 1) * bq, :], kb, (((1,), (1,)), ((), ())),
                               preferred_element_type=F32)

    def first(s, start, n):
        m = jnp.max(s, axis=-1, keepdims=True)
        p = jnp.exp2(s - m)
        l = jnp.sum(p, axis=-1, keepdims=True)
        return m, l, jnp.dot(p.astype(BF16), v_ref[0, pl.ds(start, n), :], preferred_element_type=F32)

    def update(carry, s, start, n):
        m, l, acc = carry
        m_new = jnp.maximum(m, jnp.max(s, axis=-1, keepdims=True))
        alpha = jnp.exp2(m - m_new)
        p = jnp.exp2(s - m_new)
        l = alpha * l + jnp.sum(p, axis=-1, keepdims=True)
        pv = jnp.dot(p.astype(BF16), v_ref[0, pl.ds(start, n), :], preferred_element_type=F32)
        return m_new, l, alpha * acc + pv

    visible = (lax.broadcasted_iota(jnp.int32, (bq, bq), 1) // ATTN_CHUNK
               <= lax.broadcasted_iota(jnp.int32, (bq, bq), 0) // ATTN_CHUNK)
    mid = pl.multiple_of(base + bq, bq)
    s_a = jnp.where(visible, scores(0, base, bq), -1e30)
    s_b = jnp.where(visible, scores(1, mid, bq), -1e30)
    s_b0 = scores(1, base, bq)
    chain_a = first(s_a, base, bq)
    chain_b = update(first(s_b, mid, bq), s_b0, base, bq)

    def body(j, carry):
        chain_a, chain_b = carry
        start = pl.multiple_of(j * bk, bk)
        s_a = scores(0, start, bk)
        s_b = scores(1, start, bk)
        return update(chain_a, s_a, start, bk), update(chain_b, s_b, start, bk)

    chain_a, chain_b = lax.fori_loop(0, i, body, (chain_a, chain_b))
    o_ref[0, :bq, :] = chain_a[2] / chain_a[1]
    o_ref[0, bq:, :] = chain_b[2] / chain_b[1]


def _flash(q, k, v):
    bsz, s, _ = q.shape
    bq = min(ATTN_CHAIN_ROWS, s // 2)
    qk_w = 2 * LANES
    return pl.pallas_call(
        functools.partial(_flash_kernel, bq=bq),
        grid=(bsz, MLA_HEADS, s // (2 * bq)),
        in_specs=[pl.BlockSpec((1, 2 * bq, qk_w), lambda b, h, i: (b, i, h)),
                  pl.BlockSpec((1, s, qk_w), lambda b, h, i: (b, 0, h)),
                  pl.BlockSpec((1, s, MLA_V), lambda b, h, i: (b, 0, h))],
        out_specs=pl.BlockSpec((1, 2 * bq, MLA_V), lambda b, h, i: (b, i, h)),
        out_shape=jax.ShapeDtypeStruct((bsz, s, MLA_HEADS * MLA_V), F32),
        compiler_params=_params(("parallel", "parallel", "arbitrary")),
        name="flash",
    )(q, k, v)


def _pair_diag(y, pairmask):
    yb = y.astype(BF16)
    return jnp.where(pairmask, jnp.concatenate([yb, yb], axis=1), jnp.zeros((), BF16))


def _unit_lower_inverse(a, eye, blockdiag, pairmask):
    mm = lambda x, yd: _bdot(x, yd)
    ad = jnp.where(blockdiag, a, 0.0)
    ao = a - ad
    ad_d = _pair_diag(ad, pairmask)
    a2 = mm(ad, ad_d)
    a2_d = _pair_diag(a2, pairmask)
    a4 = mm(a2, a2_d)
    a4_d = _pair_diag(a4, pairmask)
    a8 = mm(a4, a4_d)
    td = eye + ad
    td = td + mm(td, a2_d)
    td = td + mm(td, a4_d)
    td = td + mm(td, _pair_diag(a8, pairmask))
    n1 = mm(td, _pair_diag(ao, pairmask))
    n1_d = _pair_diag(n1, pairmask)
    n2 = mm(n1, n1_d)
    x = eye + n1
    x = x + mm(x, _pair_diag(n2, pairmask))
    return mm(x, _pair_diag(td, pairmask))


def _rwkv_kernel(p_ref, pprev_ref, mix_ref, w0_ref, a0_ref, kk_ref, ka_ref, rk_ref, lnw_ref, lnb_ref,
                 wl_ref, g2_ref, ones_ref, tri_ref, o_ref, state_ref, y_ref, *, nchunks):
    t = pl.program_id(1)
    nl = RWKV_CHUNK
    hd = RWKV_HEAD
    c_w = o_ref.shape[-1]
    nheads = c_w // hd

    @pl.when(t == 0)
    def _():
        state_ref[...] = jnp.zeros_like(state_ref)

    p = p_ref[0]
    prev_last = jnp.where(t == 0, 0.0, pprev_ref[0][7:8, :])
    rowi = lax.broadcasted_iota(jnp.int32, p.shape, 0)
    p_prev = jnp.where(rowi == 0, prev_last, pltpu.roll(p, 1, axis=0))
    p = p + (p_prev - p) * mix_ref[...]

    r = p[:, :c_w]
    k = p[:, c_w:2 * c_w]
    v = p[:, 2 * c_w:3 * c_w]
    xwa = p[:, 3 * c_w:3 * c_w + LANES]
    xg = p[:, 3 * c_w + LANES:]

    def segsum(x):
        halves = []
        for o in range(0, c_w, 2 * LANES):
            hi, lo = _split2(x[:, o:o + 2 * LANES])
            halves.append(jnp.dot(hi, ones_ref[...], preferred_element_type=F32)
                          + jnp.dot(lo, ones_ref[...], preferred_element_type=F32))
        return jnp.concatenate(halves, axis=1)

    lane = lax.broadcasted_iota(jnp.int32, xwa.shape, 1)
    wa = _dot(jnp.where(lane < DECAY_LORA, jnp.tanh(xwa), xwa), wl_ref[...])
    z = w0_ref[...] + wa[:, :c_w]
    lw = -EXP_NEG_HALF * _sigmoid(z)
    a = _sigmoid(a0_ref[...] + wa[:, c_w:])
    g = _dot(_sigmoid(xg), g2_ref[...])
    kk = k * kk_ref[...]
    kk = kk * lax.rsqrt(jnp.maximum(segsum(kk * kk), 1e-24))
    k2 = k * (1.0 + (a - 1.0) * ka_ref[...])
    bb = kk * a

    tri = tri_ref[...]
    grp = tri.shape[0]
    parts = _split3(lw)
    c = jnp.concatenate(
        [sum(jnp.dot(tri, part[o:o + grp], preferred_element_type=F32) for part in parts)
         for o in range(0, lw.shape[0], grp)], axis=0)
    rows = lambda x, j: x[nl * j:nl * (j + 1)]
    e_last = [jnp.exp(rows(c, j)[nl - 1:nl, :]) for j in range(nchunks)]
    e_neg = jnp.exp(-c)
    e_rem = jnp.concatenate([e_last[j] * rows(e_neg, j) for j in range(nchunks)], axis=0)
    a_t = -kk * jnp.exp(c - lw)
    r_t = r * jnp.exp(c)
    b_t = bb * e_neg
    k_t = k2 * e_neg
    b_p = bb * e_rem
    k_p = k2 * e_rem

    pw = 2 * hd
    npairs = nheads // 2

    def pairs(x, j):
        xj = rows(x, j) if x.shape[0] > 1 else x
        return jnp.stack([xj[:, pw * q:pw * (q + 1)] for q in range(npairs)], axis=0)

    def all_pairs(x):
        return jnp.concatenate([pairs(x, j) for j in range(nchunks)], axis=0)

    ri = lax.broadcasted_iota(jnp.int32, (1, nl, pw), 1)
    ci = lax.broadcasted_iota(jnp.int32, (1, nl, pw), 2) % hd
    strict = ci < ri
    incl = ci <= ri
    eye = (ci == ri).astype(F32)
    blockdiag = (ri // INV_BLOCK) == (ci // INV_BLOCK)
    pairmask = (lax.broadcasted_iota(jnp.int32, (1, pw, pw), 1) // hd
                == lax.broadcasted_iota(jnp.int32, (1, pw, pw), 2) // hd)

    a_p = all_pairs(a_t)
    r_p = all_pairs(r_t)
    ar = jnp.concatenate([a_p, r_p], axis=1)
    g_bk = _bdot_nt(ar, jnp.concatenate([_pair_diag(all_pairs(b_t), pairmask),
                                         _pair_diag(all_pairs(k_t), pairmask)], axis=1))
    gb, gk = g_bk[:, :, :pw], g_bk[:, :, pw:]
    a_rb = jnp.where(incl, gb[:, nl:], 0.0)
    a_ak = jnp.where(strict, gk[:, :nl], 0.0)
    a_rk = jnp.where(incl, gk[:, nl:], 0.0)
    tinv = _unit_lower_inverse(jnp.where(strict, gb[:, :nl], 0.0), eye, blockdiag, pairmask)
    v_p = all_pairs(v)
    v_d = _pair_diag(v_p, pairmask)
    akv = _bdot(a_ak, v_d)
    rkv = _bdot(a_rk, v_d)
    kpv = _bdot_tn(v_p, all_pairs(k_p))
    b_pp = all_pairs(b_p).astype(BF16)
    ta_u0 = _bdot(tinv, jnp.concatenate([_pair_diag(a_p, pairmask), _pair_diag(akv, pairmask)], axis=2))
    u0 = ta_u0[:, :, pw:]
    tar = jnp.concatenate([ta_u0[:, :, :pw], r_p], axis=1).astype(BF16)

    st = state_ref[...]
    for j in range(nchunks):
        sj = slice(npairs * j, npairs * (j + 1))
        sz = _bdot_nt(tar[sj], st)
        u = sz[:, :nl] + u0[sj]
        y_j = sz[:, nl:] + _bdot(a_rb[sj], _pair_diag(u, pairmask)) + rkv[sj]
        for q in range(npairs):
            y_ref[nl * j:nl * (j + 1), pw * q:pw * (q + 1)] = y_j[q]
        st = jnp.where(pairmask, st * pairs(e_last[j], 0) + _bdot_tn(u, b_pp[sj]) + kpv[sj], 0.0)
    state_ref[...] = st

    y = y_ref[...]
    inv_n = 1.0 / hd
    mu = segsum(y) * inv_n
    d = y - mu
    var = segsum(d * d) * inv_n
    yn = d * lax.rsqrt(var + GN_EPS) * lnw_ref[...] + lnb_ref[...]
    bonus = segsum(r * k2 * rk_ref[...]) * v
    o_ref[0] = (yn + bonus) * g


def _rwkv(p_rw, mix, w0, a0, k_k, k_a, r_k, ln_w, ln_b, wl, g2):
    bsz, s, n_rw = p_rw.shape
    c_w = w0.shape[-1]
    nchunks = min(RWKV_CHUNKS_PER_STEP, s // RWKV_CHUNK)
    nl = RWKV_CHUNK * nchunks
    nheads = c_w // RWKV_HEAD
    seg = jnp.arange(2 * LANES) // RWKV_HEAD
    ones_bd = (seg[:, None] == seg[None, :]).astype(BF16)
    pos = jnp.arange(min(nl, RWKV_CUMSUM_ROWS))
    tri = ((pos[None, :] <= pos[:, None])
           & (pos[None, :] // RWKV_CHUNK == pos[:, None] // RWKV_CHUNK)).astype(BF16)
    vec = lambda x: x.reshape(1, -1)
    sub = nl // 8
    return pl.pallas_call(
        functools.partial(_rwkv_kernel, nchunks=nchunks),
        grid=(bsz, s // nl),
        in_specs=[pl.BlockSpec((1, nl, n_rw), lambda b, t: (b, t, 0)),
                  pl.BlockSpec((1, 8, n_rw), lambda b, t: (b, jnp.maximum(t * sub - 1, 0), 0)),
                  _const_spec((1, n_rw))] + [_const_spec((1, c_w))] * 7
                 + [_const_spec(wl.shape), _const_spec(g2.shape), _const_spec(ones_bd.shape),
                    _const_spec(tri.shape)],
        out_specs=pl.BlockSpec((1, nl, c_w), lambda b, t: (b, t, 0)),
        out_shape=jax.ShapeDtypeStruct((bsz, s, c_w), F32),
        scratch_shapes=[pltpu.VMEM((nheads // 2, 2 * RWKV_HEAD, 2 * RWKV_HEAD), F32),
                        pltpu.VMEM((nl, c_w), F32)],
        compiler_params=_params(("parallel", "arbitrary")),
        name="rwkv",
    )(p_rw, p_rw, vec(mix), vec(w0), vec(a0), vec(k_k), vec(k_a), vec(r_k), vec(ln_w), vec(ln_b),
      wl, g2, ones_bd, tri)


def _swap_halves(w):
    half = w.shape[-1] // 2
    return jnp.concatenate([w[..., half:], w[..., :half]], axis=-1)


def kernel(x, c, positions, w_mod, b_mod, ffn1_norm_g, ffn1_w_gate, ffn1_w_up, ffn1_w_down, mix_norm_g, w_in, q_norm_g, w_uq, kv_norm_g, w_ukv, attn_out_norm_g, rwkv_shift_mix, rwkv_w0, rwkv_w2, rwkv_a0, rwkv_a2, rwkv_g2, rwkv_k_k, rwkv_k_a, rwkv_r_k, rwkv_ln_w, rwkv_ln_b, w_out, ffn2_norm_g, ffn2_w_gate, ffn2_w_up, ffn2_w_down, final_norm_g):
    depth = w_mod.shape[0]
    bsz, s, d = x.shape
    nh, nope, rope_w, vw = MLA_HEADS, MLA_NOPE, MLA_ROPE, MLA_V
    mla_cols = Q_LORA + KV_LORA + rope_w
    mla_w = MLA_HEADS * MLA_V

    half = rope_w // 2
    inv_freq = ROPE_BASE ** (-jnp.arange(half, dtype=F32) / half)
    pos_dense = jnp.repeat(positions.astype(F32), half, axis=1).reshape(bsz, s * half // LANES, LANES)
    ang = pos_dense * jnp.tile(inv_freq, LANES // half)
    cos, sin = lax.optimization_barrier((jnp.cos(ang), jnp.sin(ang)))
    cos, sin = cos.reshape(bsz, s, half), sin.reshape(bsz, s, half)
    cs = jnp.concatenate([cos, cos, -sin, sin], axis=-1)

    h = x
    for l in range(depth):
        w_kr = w_in[l][:, Q_LORA + KV_LORA:mla_cols]
        w_in_l = jnp.concatenate([w_in[l][:, :mla_cols], _swap_halves(w_kr), w_in[l][:, mla_cols:]],
                                 axis=1).astype(BF16)
        n_mla = mla_cols + rope_w
        wq = w_uq[l].reshape(Q_LORA, nh, nope + rope_w)
        wq_rope = wq[:, :, nope:]
        wq_l = jnp.concatenate(
            [wq[:, :, :nope].reshape(Q_LORA, nh * nope),
             jnp.concatenate([wq_rope, _swap_halves(wq_rope)], axis=-1).reshape(Q_LORA, nh * 2 * rope_w)],
            axis=1).astype(BF16)
        wkv = w_ukv[l].reshape(KV_LORA, nh, nope + vw)
        wkv_l = jnp.concatenate([wkv[:, :, :nope].reshape(KV_LORA, nh * nope),
                                 wkv[:, :, nope:].reshape(KV_LORA, nh * vw)], axis=1).astype(BF16)
        c_w = rwkv_w0.shape[-1]
        zeros = jnp.zeros((DECAY_LORA, c_w), F32)
        wl = jnp.concatenate([jnp.concatenate([rwkv_w2[l], zeros], axis=1),
                              jnp.concatenate([zeros, rwkv_a2[l]], axis=1)], axis=0).astype(BF16)

        mod = _mod(c, w_mod[l], b_mod[l]).reshape(bsz, N_MOD, d)
        h = _ffn(h, mod, ffn1_norm_g[l], ffn1_w_gate[l].astype(BF16), ffn1_w_up[l].astype(BF16),
                 ffn1_w_down[l].astype(BF16), mod_rows=(0, 1, 2))
        p_rw, q, k, v = _inproj(h, mod, mix_norm_g[l], w_in_l, n_mla, cs, q_norm_g[l], kv_norm_g[l],
                                wq_l, wkv_l)
        y_a = _flash(q, k, v)
        y_b = _rwkv(p_rw, rwkv_shift_mix[l], rwkv_w0[l], rwkv_a0[l], rwkv_k_k[l], rwkv_k_a[l],
                    rwkv_r_k[l].reshape(-1), rwkv_ln_w[l], rwkv_ln_b[l], wl, rwkv_g2[l].astype(BF16))
        mix = (y_a, y_b, attn_out_norm_g[l], w_out[l][:mla_w].astype(BF16), w_out[l][mla_w:].astype(BF16))
        h = _ffn(h, mod, ffn2_norm_g[l], ffn2_w_gate[l].astype(BF16), ffn2_w_up[l].astype(BF16),
                 ffn2_w_down[l].astype(BF16), mod_rows=(6, 7, 8, 5), mix=mix,
                 final_g=final_norm_g if l == depth - 1 else None)
    return h
```

```python
import functools

import jax
import jax.numpy as jnp
from jax import lax
from jax.experimental import pallas as pl
from jax.experimental.pallas import tpu as pltpu

F32 = jnp.float32
BF16 = jnp.bfloat16

MLA_HEADS = 4
MLA_NOPE = 128
MLA_ROPE = 64
MLA_V = 128
Q_LORA = 384
KV_LORA = 256
ROPE_BASE = 10000.0
RWKV_HEAD = 64
DECAY_LORA = 64
AAA_LORA = 64
GATE_LORA = 128
NORM_EPS = 1e-6
GN_EPS = 64e-5
N_MOD = 9
ATTN_CHUNK = 64
LOG2_E = 1.4426950408889634
EXP_NEG_HALF = 0.6065306597126334

LANES = 128
VMEM_LIMIT_BYTES = 56 * 1024 * 1024

FFN_ROWS = 512
ATTN_CHAIN_ROWS = 512
RWKV_CHUNK = 64
RWKV_CHUNKS_PER_STEP = 1
RWKV_CUMSUM_ROWS = 256
INV_BLOCK = 16


def _dot(a, b):
    return jnp.dot(a.astype(BF16), b.astype(BF16), preferred_element_type=F32)


def _batched(a, b, ca, cb):
    return lax.dot_general(a.astype(BF16), b.astype(BF16), (((ca,), (cb,)), ((0,), (0,))),
                           preferred_element_type=F32)


def _bdot(a, b):
    return _batched(a, b, 2, 1)


def _bdot_nt(a, b):
    return _batched(a, b, 2, 2)


def _bdot_tn(a, b):
    return _batched(a, b, 1, 1)


def _split2(x):
    hi = x.astype(BF16)
    lo = (x - hi.astype(F32)).astype(BF16)
    return hi, lo


def _split3(x):
    hi = x.astype(BF16)
    r1 = x - hi.astype(F32)
    mid = r1.astype(BF16)
    lo = (r1 - mid.astype(F32)).astype(BF16)
    return hi, mid, lo


def _sigmoid(x):
    return 1.0 / (1.0 + jnp.exp(-x))


def _rms(x, g):
    return x * lax.rsqrt(jnp.mean(x * x, axis=-1, keepdims=True) + NORM_EPS) * g


def _const_spec(shape):
    nd = len(shape)
    return pl.BlockSpec(shape, lambda *_: (0,) * nd, pipeline_mode=pl.Buffered(1))


def _params(semantics):
    return pltpu.CompilerParams(dimension_semantics=semantics, vmem_limit_bytes=VMEM_LIMIT_BYTES)


def _mod_kernel(c_ref, w_ref, b_ref, o_ref):
    c = c_ref[...]
    c_act = c * _sigmoid(c)
    o_ref[...] = _dot(c_act, w_ref[...]) + b_ref[...]


def _mod(c, w_mod, b_mod):
    bsz, d = c.shape
    n = w_mod.shape[1]
    tn = d
    return pl.pallas_call(
        _mod_kernel,
        grid=(n // tn,),
        in_specs=[pl.BlockSpec((bsz, d), lambda j: (0, 0)),
                  pl.BlockSpec((d, tn), lambda j: (0, j)),
                  pl.BlockSpec((1, tn), lambda j: (0, j))],
        out_specs=pl.BlockSpec((bsz, tn), lambda j: (0, j)),
        out_shape=jax.ShapeDtypeStruct((bsz, n), F32),
        compiler_params=_params(("arbitrary",)),
        name="mod",
    )(c, w_mod, b_mod.reshape(1, n))


def _ffn_kernel(*refs, mod_rows, pre_mix, final_norm):
    it = iter(refs)
    h_ref, mod_ref, ng_ref, wg_ref, wu_ref, wd_ref = (next(it) for _ in range(6))
    if pre_mix:
        ya_ref, yb_ref, ang_ref, woa_ref, wob_ref = (next(it) for _ in range(5))
    if final_norm:
        fng_ref = next(it)
    o_ref = next(it)

    h = h_ref[0]
    mod = mod_ref[0]
    row = lambda i: mod[i:i + 1, :]
    if pre_mix:
        ya = _rms(ya_ref[0], ang_ref[...])
        y = _dot(ya, woa_ref[...]) + _dot(yb_ref[0], wob_ref[...])
        h = h + row(mod_rows[3]) * y
    sh, sc, gt = row(mod_rows[0]), row(mod_rows[1]), row(mod_rows[2])
    u = (_rms(h, ng_ref[...]) * (1.0 + sc) + sh).astype(BF16)
    g = jnp.dot(u, wg_ref[...], preferred_element_type=F32)
    up = jnp.dot(u, wu_ref[...], preferred_element_type=F32)
    act = (g * _sigmoid(g) * up).astype(BF16)
    f = jnp.dot(act, wd_ref[...], preferred_element_type=F32)
    h = h + 0.5 * gt * f
    if final_norm:
        h = _rms(h, fng_ref[...])
    o_ref[0] = h


def _ffn(h, mod, ng, wg, wu, wd, *, mod_rows, mix=None, final_g=None):
    bsz, s, d = h.shape
    f = wg.shape[1]
    tm = min(FFN_ROWS, s)
    row_spec = lambda w: pl.BlockSpec((1, tm, w), lambda b, i: (b, i, 0))
    in_specs = [row_spec(d),
                pl.BlockSpec((1, N_MOD, d), lambda b, i: (b, 0, 0)),
                _const_spec((1, d)), _const_spec((d, f)), _const_spec((d, f)), _const_spec((f, d))]
    args = [h, mod, ng.reshape(1, d), wg, wu, wd]
    if mix is not None:
        ya, yb, ang, woa, wob = mix
        wa, wb = ya.shape[-1], yb.shape[-1]
        in_specs += [row_spec(wa), row_spec(wb), _const_spec((1, wa)),
                     _const_spec((wa, d)), _const_spec((wb, d))]
        args += [ya, yb, ang.reshape(1, wa), woa, wob]
    if final_g is not None:
        in_specs.append(_const_spec((1, d)))
        args.append(final_g.reshape(1, d))
    kern = functools.partial(_ffn_kernel, mod_rows=mod_rows, pre_mix=mix is not None,
                             final_norm=final_g is not None)
    return pl.pallas_call(
        kern,
        grid=(bsz, s // tm),
        in_specs=in_specs,
        out_specs=row_spec(d),
        out_shape=jax.ShapeDtypeStruct((bsz, s, d), F32),
        compiler_params=_params(("parallel", "parallel")),
        name="ffn_mix" if mix is not None else "ffn",
    )(*args)


def _inproj_kernel(h_ref, mod_ref, ng_ref, w_ref, pos_ref, cs_ref, qg_ref, kvg_ref, wq_ref, wkv_ref,
                   pr_ref, q_ref, k_ref, v_ref, *, n_mla, scale):
    mod = mod_ref[0]
    u = _rms(h_ref[0], ng_ref[...]) * (1.0 + mod[4:5, :]) + mod[3:4, :]
    proj = _dot(u, w_ref[...])
    pr_ref[0] = proj[:, n_mla:]
    p = proj[:, :n_mla]
    nh, nope = MLA_HEADS, MLA_NOPE
    ang = pos_ref[0] * cs_ref[0:1, :]
    lane = lax.broadcasted_iota(jnp.int32, ang.shape, 1)
    low = lane < MLA_ROPE
    cs = jnp.where(low, jnp.cos(ang), cs_ref[1:2, :] * jnp.sin(ang))

    def rope(xpair):
        prod = xpair * cs
        return jnp.where(low, prod + pltpu.roll(prod, MLA_ROPE, axis=1), 0.0)

    q = _dot(_rms(p[:, :Q_LORA], qg_ref[...]), wq_ref[...]) * scale
    kv = _dot(_rms(p[:, Q_LORA:Q_LORA + KV_LORA], kvg_ref[...]), wkv_ref[...])
    k_rope = rope(p[:, Q_LORA + KV_LORA:]).astype(BF16)
    for h in range(nh):
        o = 2 * LANES * h
        q_ref[0, :, o:o + nope] = q[:, nope * h:nope * (h + 1)].astype(BF16)
        q_ref[0, :, o + nope:o + 2 * LANES] = rope(
            q[:, nh * nope + LANES * h:nh * nope + LANES * (h + 1)]).astype(BF16)
        k_ref[0, :, o:o + nope] = kv[:, nope * h:nope * (h + 1)].astype(BF16)
        k_ref[0, :, o + nope:o + 2 * LANES] = k_rope
    v_ref[0] = kv[:, nh * nope:].astype(BF16)


def _inproj(h, mod, ng, w, n_mla, pos, cs, qg, kvg, wq, wkv):
    bsz, s, d = h.shape
    n = w.shape[1]
    tm = min(FFN_ROWS, s)
    row_spec = lambda wd_: pl.BlockSpec((1, tm, wd_), lambda b, i: (b, i, 0))
    qk_w = MLA_HEADS * 2 * LANES
    v_w = MLA_HEADS * MLA_V
    return pl.pallas_call(
        functools.partial(_inproj_kernel, n_mla=n_mla, scale=(MLA_NOPE + MLA_ROPE) ** -0.5 * LOG2_E),
        grid=(bsz, s // tm),
        in_specs=[row_spec(d), pl.BlockSpec((1, N_MOD, d), lambda b, i: (b, 0, 0)),
                  _const_spec((1, d)), _const_spec((d, n)), row_spec(1), _const_spec((2, LANES)),
                  _const_spec((1, Q_LORA)), _const_spec((1, KV_LORA)),
                  _const_spec(wq.shape), _const_spec(wkv.shape)],
        out_specs=[row_spec(n - n_mla), row_spec(qk_w), row_spec(qk_w), row_spec(v_w)],
        out_shape=[jax.ShapeDtypeStruct((bsz, s, n - n_mla), F32),
                   jax.ShapeDtypeStruct((bsz, s, qk_w), BF16),
                   jax.ShapeDtypeStruct((bsz, s, qk_w), BF16),
                   jax.ShapeDtypeStruct((bsz, s, v_w), BF16)],
        compiler_params=_params(("parallel", "parallel")),
        name="inproj",
    )(h, mod, ng.reshape(1, d), w, pos, cs, qg.reshape(1, -1), kvg.reshape(1, -1), wq, wkv)


def _flash_kernel(q_ref, k_ref, v_ref, o_ref, *, bq):
    i = pl.program_id(2)
    bk = 2 * bq
    base = pl.multiple_of(i * bk, bk)

    def scores(chain, start, n):
        kb = k_ref[0, pl.ds(start, n), :]
        return lax.dot_general(q_ref[0, chain * bq:(chain + 1) * bq, :], kb, (((1,), (1,)), ((), ())),
                               preferred_element_type=F32)

    def first(s, start, n):
        m = jnp.max(s, axis=-1, keepdims=True)
        p = jnp.exp2(s - m)
        l = jnp.sum(p, axis=-1, keepdims=True)
        return m, l, jnp.dot(p.astype(BF16), v_ref[0, pl.ds(start, n), :], preferred_element_type=F32)

    def update(carry, s, start, n):
        m, l, acc = carry
        m_new = jnp.maximum(m, jnp.max(s, axis=-1, keepdims=True))
        alpha = jnp.exp2(m - m_new)
        p = jnp.exp2(s - m_new)
        l = alpha * l + jnp.sum(p, axis=-1, keepdims=True)
        pv = jnp.dot(p.astype(BF16), v_ref[0, pl.ds(start, n), :], preferred_element_type=F32)
        return m_new, l, alpha * acc + pv

    visible = (lax.broadcasted_iota(jnp.int32, (bq, bq), 1) // ATTN_CHUNK
               <= lax.broadcasted_iota(jnp.int32, (bq, bq), 0) // ATTN_CHUNK)
    mid = pl.multiple_of(base + bq, bq)
    s_a = jnp.where(visible, scores(0, base, bq), -1e30)
    s_b = jnp.where(visible, scores(1, mid, bq), -1e30)
    s_b0 = scores(1, base, bq)
    chain_a = first(s_a, base, bq)
    chain_b = update(first(s_b, mid, bq), s_b0, base, bq)

    def body(j, carry):
        chain_a, chain_b = carry
        start = pl.multiple_of(j * bk, bk)
        s_a = scores(0, start, bk)
        s_b = scores(1, start, bk)
        return update(chain_a, s_a, start, bk), update(chain_b, s_b, start, bk)

    chain_a, chain_b = lax.fori_loop(0, i, body, (chain_a, chain_b))
    o_ref[0, :bq, :] = chain_a[2] / chain_a[1]
    o_ref[0, bq:, :] = chain_b[2] / chain_b[1]


def _flash(q, k, v):
    bsz, s, _ = q.shape
    bq = min(ATTN_CHAIN_ROWS, s // 2)
    qk_w = 2 * LANES
    return pl.pallas_call(
        functools.partial(_flash_kernel, bq=bq),
        grid=(bsz, MLA_HEADS, s // (2 * bq)),
        in_specs=[pl.BlockSpec((1, 2 * bq, qk_w), lambda b, h, i: (b, i, h)),
                  pl.BlockSpec((1, s, qk_w), lambda b, h, i: (b, 0, h)),
                  pl.BlockSpec((1, s, MLA_V), lambda b, h, i: (b, 0, h))],
        out_specs=pl.BlockSpec((1, 2 * bq, MLA_V), lambda b, h, i: (b, i, h)),
        out_shape=jax.ShapeDtypeStruct((bsz, s, MLA_HEADS * MLA_V), F32),
        compiler_params=_params(("parallel", "parallel", "arbitrary")),
        name="flash",
    )(q, k, v)


def _pair_diag(y, pairmask):
    yb = y.astype(BF16)
    return jnp.where(pairmask, jnp.concatenate([yb, yb], axis=1), jnp.zeros((), BF16))


def _unit_lower_inverse(a, eye, blockdiag, pairmask):
    mm = lambda x, yd: _bdot(x, yd)
    ad = jnp.where(blockdiag, a, 0.0)
    ao = a - ad
    ad_d = _pair_diag(ad, pairmask)
    a2 = mm(ad, ad_d)
    a2_d = _pair_diag(a2, pairmask)
    a4 = mm(a2, a2_d)
    a4_d = _pair_diag(a4, pairmask)
    a8 = mm(a4, a4_d)
    td = eye + ad
    td = td + mm(td, a2_d)
    td = td + mm(td, a4_d)
    td = td + mm(td, _pair_diag(a8, pairmask))
    n1 = mm(td, _pair_diag(ao, pairmask))
    n1_d = _pair_diag(n1, pairmask)
    n2 = mm(n1, n1_d)
    x = eye + n1
    x = x + mm(x, _pair_diag(n2, pairmask))
    return mm(x, _pair_diag(td, pairmask))


def _rwkv_kernel(p_ref, pprev_ref, mix_ref, w0_ref, a0_ref, kk_ref, ka_ref, rk_ref, lnw_ref, lnb_ref,
                 wl_ref, g2_ref, ones_ref, tri_ref, o_ref, state_ref, y_ref, *, nseq, nchunks):
    t = pl.program_id(0)
    nl = RWKV_CHUNK
    hd = RWKV_HEAD
    c_w = o_ref.shape[-1]
    nheads = c_w // hd

    @pl.when(t == 0)
    def _():
        state_ref[...] = jnp.zeros_like(state_ref)

    rowi = lax.broadcasted_iota(jnp.int32, p_ref.shape[1:], 0)
    shifted = []
    for b in range(nseq):
        pb = p_ref[b]
        prev_last = jnp.where(t == 0, 0.0, pprev_ref[b][7:8, :])
        pb_prev = jnp.where(rowi == 0, prev_last, pltpu.roll(pb, 1, axis=0))
        shifted.append(pb + (pb_prev - pb) * mix_ref[...])
    p = jnp.concatenate(shifted, axis=0)

    r = p[:, :c_w]
    k = p[:, c_w:2 * c_w]
    v = p[:, 2 * c_w:3 * c_w]
    xwa = p[:, 3 * c_w:3 * c_w + LANES]
    xg = p[:, 3 * c_w + LANES:]

    def segsum(x, terms):
        halves = []
        for o in range(0, c_w, 2 * LANES):
            xo = x[:, o:o + 2 * LANES]
            pieces = _split2(xo) if terms == 2 else (xo.astype(BF16),)
            halves.append(sum(jnp.dot(pc, ones_ref[...], preferred_element_type=F32) for pc in pieces))
        return jnp.concatenate(halves, axis=1)

    def rsqrt(x):
        return jnp.exp2(-0.5 * jnp.log2(x))

    def sigmoid(x):
        return 0.5 * jnp.tanh(0.5 * x) + 0.5

    lane = lax.broadcasted_iota(jnp.int32, xwa.shape, 1)
    wa = _dot(jnp.where(lane < DECAY_LORA, jnp.tanh(xwa), xwa), wl_ref[...])
    z = w0_ref[...] + wa[:, :c_w]
    lw = -EXP_NEG_HALF * sigmoid(z)
    a = sigmoid(a0_ref[...] + wa[:, c_w:])
    g = _dot(sigmoid(xg), g2_ref[...])
    kk = k * kk_ref[...]
    kk = kk * rsqrt(jnp.maximum(segsum(kk * kk, 1), 1e-24))
    k2 = k * (1.0 + (a - 1.0) * ka_ref[...])
    bb = kk * a

    tri = tri_ref[...]
    grp = tri.shape[0]
    parts = _split3(lw)
    c = jnp.concatenate(
        [sum(jnp.dot(tri, part[o:o + grp], preferred_element_type=F32) for part in parts)
         for o in range(0, lw.shape[0], grp)], axis=0)
    slots = [(b, j) for b in range(nseq) for j in range(nchunks)]
    rows = lambda x, b, j: x[nl * (b * nchunks + j):nl * (b * nchunks + j + 1)]
    e_last = {bj: jnp.exp(rows(c, *bj)[nl - 1:nl, :]) for bj in slots}
    e_neg = jnp.exp(-c)
    e_rem = jnp.concatenate([e_last[bj] * rows(e_neg, *bj) for bj in slots], axis=0)
    a_t = -kk * jnp.exp(c - lw)
    r_t = r * jnp.exp(c)
    b_t = bb * e_neg
    k_t = k2 * e_neg
    b_p = bb * e_rem
    k_p = k2 * e_rem

    pw = 2 * hd
    npairs = nheads // 2

    def pairs(x):
        return jnp.stack([x[:, pw * q:pw * (q + 1)] for q in range(npairs)], axis=0)

    def all_pairs(x):
        return jnp.concatenate([pairs(rows(x, b, j)) for j in range(nchunks) for b in range(nseq)], axis=0)

    ri = lax.broadcasted_iota(jnp.int32, (1, nl, pw), 1)
    ci = lax.broadcasted_iota(jnp.int32, (1, nl, pw), 2) % hd
    strict = ci < ri
    incl = ci <= ri
    eye = (ci == ri).astype(F32)
    blockdiag = (ri // INV_BLOCK) == (ci // INV_BLOCK)
    pairmask = (lax.broadcasted_iota(jnp.int32, (1, pw, pw), 1) // hd
                == lax.broadcasted_iota(jnp.int32, (1, pw, pw), 2) // hd)

    a_p = all_pairs(a_t)
    r_p = all_pairs(r_t)
    ar = jnp.concatenate([a_p, r_p], axis=1)
    g_bk = _bdot_nt(ar, jnp.concatenate([_pair_diag(all_pairs(b_t), pairmask),
                                         _pair_diag(all_pairs(k_t), pairmask)], axis=1))
    gb, gk = g_bk[:, :, :pw], g_bk[:, :, pw:]
    a_rb = jnp.where(incl, gb[:, nl:], 0.0)
    a_ak = jnp.where(strict, gk[:, :nl], 0.0)
    a_rk = jnp.where(incl, gk[:, nl:], 0.0)
    tinv = _unit_lower_inverse(jnp.where(strict, gb[:, :nl], 0.0), eye, blockdiag, pairmask)
    v_p = all_pairs(v)
    v_d = _pair_diag(v_p, pairmask)
    akv = _bdot(a_ak, v_d)
    rkv = _bdot(a_rk, v_d)
    kpv = _bdot_tn(v_p, all_pairs(k_p))
    b_pp = all_pairs(b_p).astype(BF16)
    ta_u0 = _bdot(tinv, jnp.concatenate([_pair_diag(a_p, pairmask), _pair_diag(akv, pairmask)], axis=2))
    u0 = ta_u0[:, :, pw:]
    tar = jnp.concatenate([ta_u0[:, :, :pw], r_p], axis=1).astype(BF16)

    st = state_ref[...]
    per_chunk = nseq * npairs
    for j in range(nchunks):
        sj = slice(per_chunk * j, per_chunk * (j + 1))
        sz = _bdot_nt(tar[sj], st)
        u = sz[:, :nl] + u0[sj]
        y_j = sz[:, nl:] + _bdot(a_rb[sj], _pair_diag(u, pairmask)) + rkv[sj]
        for b in range(nseq):
            r0 = nl * (b * nchunks + j)
            for q in range(npairs):
                y_ref[r0:r0 + nl, pw * q:pw * (q + 1)] = y_j[b * npairs + q]
        decay = jnp.concatenate([pairs(e_last[(b, j)]) for b in range(nseq)], axis=0)
        st = jnp.where(pairmask, st * decay + _bdot_tn(u, b_pp[sj]) + kpv[sj], 0.0)
    state_ref[...] = st

    y = y_ref[...]
    inv_n = 1.0 / hd
    mu = segsum(y, 2) * inv_n
    d = y - mu
    var = segsum(d * d, 1) * inv_n
    yn = d * rsqrt(var + GN_EPS) * lnw_ref[...] + lnb_ref[...]
    bonus = segsum(r * k2 * rk_ref[...], 1) * v
    o_ref[...] = ((yn + bonus) * g).reshape(o_ref.shape)


def _rwkv(p_rw, mix, w0, a0, k_k, k_a, r_k, ln_w, ln_b, wl, g2):
    bsz, s, n_rw = p_rw.shape
    c_w = w0.shape[-1]
    nchunks = min(RWKV_CHUNKS_PER_STEP, s // RWKV_CHUNK)
    nl = RWKV_CHUNK * nchunks
    nheads = c_w // RWKV_HEAD
    seg = jnp.arange(2 * LANES) // RWKV_HEAD
    ones_bd = (seg[:, None] == seg[None, :]).astype(BF16)
    pos = jnp.arange(min(bsz * nl, RWKV_CUMSUM_ROWS))
    tri = ((pos[None, :] <= pos[:, None])
           & (pos[None, :] // RWKV_CHUNK == pos[:, None] // RWKV_CHUNK)).astype(BF16)
    vec = lambda x: x.reshape(1, -1)
    sub = nl // 8
    return pl.pallas_call(
        functools.partial(_rwkv_kernel, nseq=bsz, nchunks=nchunks),
        grid=(s // nl,),
        in_specs=[pl.BlockSpec((bsz, nl, n_rw), lambda t: (0, t, 0)),
                  pl.BlockSpec((bsz, 8, n_rw), lambda t: (0, jnp.maximum(t * sub - 1, 0), 0)),
                  _const_spec((1, n_rw))] + [_const_spec((1, c_w))] * 7
                 + [_const_spec(wl.shape), _const_spec(g2.shape), _const_spec(ones_bd.shape),
                    _const_spec(tri.shape)],
        out_specs=pl.BlockSpec((bsz, nl, c_w), lambda t: (0, t, 0)),
        out_shape=jax.ShapeDtypeStruct((bsz, s, c_w), F32),
        scratch_shapes=[pltpu.VMEM((bsz * (nheads // 2), 2 * RWKV_HEAD, 2 * RWKV_HEAD), F32),
                        pltpu.VMEM((bsz * nl, c_w), F32)],
        compiler_params=_params(("arbitrary",)),
        name="rwkv",
    )(p_rw, p_rw, vec(mix), vec(w0), vec(a0), vec(k_k), vec(k_a), vec(r_k), vec(ln_w), vec(ln_b),
      wl, g2, ones_bd, tri)


def _swap_halves(w):
    half = w.shape[-1] // 2
    return jnp.concatenate([w[..., half:], w[..., :half]], axis=-1)


def kernel(x, c, positions, w_mod, b_mod, ffn1_norm_g, ffn1_w_gate, ffn1_w_up, ffn1_w_down, mix_norm_g, w_in, q_norm_g, w_uq, kv_norm_g, w_ukv, attn_out_norm_g, rwkv_shift_mix, rwkv_w0, rwkv_w2, rwkv_a0, rwkv_a2, rwkv_g2, rwkv_k_k, rwkv_k_a, rwkv_r_k, rwkv_ln_w, rwkv_ln_b, w_out, ffn2_norm_g, ffn2_w_gate, ffn2_w_up, ffn2_w_down, final_norm_g):
    depth = w_mod.shape[0]
    bsz, s, d = x.shape
    nh, nope, rope_w, vw = MLA_HEADS, MLA_NOPE, MLA_ROPE, MLA_V
    mla_cols = Q_LORA + KV_LORA + rope_w
    mla_w = MLA_HEADS * MLA_V

    half = rope_w // 2
    inv_freq = ROPE_BASE ** (-jnp.arange(half, dtype=F32) / half)
    sine_sign = jnp.where(jnp.arange(LANES) < 3 * half, -1.0, 1.0).astype(F32)
    cs = jnp.stack([jnp.tile(inv_freq, LANES // half), sine_sign])
    pos = positions.astype(F32)[..., None]

    h = x
    for l in range(depth):
        w_kr = w_in[l][:, Q_LORA + KV_LORA:mla_cols]
        w_in_l = jnp.concatenate([w_in[l][:, :mla_cols], _swap_halves(w_kr), w_in[l][:, mla_cols:]],
                                 axis=1).astype(BF16)
        n_mla = mla_cols + rope_w
        wq = w_uq[l].reshape(Q_LORA, nh, nope + rope_w)
        wq_rope = wq[:, :, nope:]
        wq_l = jnp.concatenate(
            [wq[:, :, :nope].reshape(Q_LORA, nh * nope),
             jnp.concatenate([wq_rope, _swap_halves(wq_rope)], axis=-1).reshape(Q_LORA, nh * 2 * rope_w)],
            axis=1).astype(BF16)
        wkv = w_ukv[l].reshape(KV_LORA, nh, nope + vw)
        wkv_l = jnp.concatenate([wkv[:, :, :nope].reshape(KV_LORA, nh * nope),
                                 wkv[:, :, nope:].reshape(KV_LORA, nh * vw)], axis=1).astype(BF16)
        c_w = rwkv_w0.shape[-1]
        zeros = jnp.zeros((DECAY_LORA, c_w), F32)
        wl = jnp.concatenate([jnp.concatenate([rwkv_w2[l], zeros], axis=1),
                              jnp.concatenate([zeros, rwkv_a2[l]], axis=1)], axis=0).astype(BF16)

        mod = _mod(c, w_mod[l], b_mod[l]).reshape(bsz, N_MOD, d)
        h = _ffn(h, mod, ffn1_norm_g[l], ffn1_w_gate[l].astype(BF16), ffn1_w_up[l].astype(BF16),
                 ffn1_w_down[l].astype(BF16), mod_rows=(0, 1, 2))
        p_rw, q, k, v = _inproj(h, mod, mix_norm_g[l], w_in_l, n_mla, pos, cs, q_norm_g[l], kv_norm_g[l],
                                wq_l, wkv_l)
        y_a = _flash(q, k, v)
        y_b = _rwkv(p_rw, rwkv_shift_mix[l], rwkv_w0[l], rwkv_a0[l], rwkv_k_k[l], rwkv_k_a[l],
                    rwkv_r_k[l].reshape(-1), rwkv_ln_w[l], rwkv_ln_b[l], wl, rwkv_g2[l].astype(BF16))
        mix = (y_a, y_b, attn_out_norm_g[l], w_out[l][:mla_w].astype(BF16), w_out[l][mla_w:].astype(BF16))
        h = _ffn(h, mod, ffn2_norm_g[l], ffn2_w_gate[l].astype(BF16), ffn2_w_up[l].astype(BF16),
                 ffn2_w_down[l].astype(BF16), mod_rows=(6, 7, 8, 5), mix=mix,
                 final_g=final_norm_g if l == depth - 1 else None)
    return h
```

```python
import functools

import jax
import jax.numpy as jnp
from jax import lax
from jax.experimental import pallas as pl
from jax.experimental.pallas import tpu as pltpu

F32 = jnp.float32
BF16 = jnp.bfloat16

MLA_HEADS = 4
MLA_NOPE = 128
MLA_ROPE = 64
MLA_V = 128
Q_LORA = 384
KV_LORA = 256
ROPE_BASE = 10000.0
RWKV_HEAD = 64
DECAY_LORA = 64
AAA_LORA = 64
GATE_LORA = 128
NORM_EPS = 1e-6
GN_EPS = 64e-5
N_MOD = 9
ATTN_CHUNK = 64
LOG2_E = 1.4426950408889634
EXP_NEG_HALF = 0.6065306597126334

LANES = 128
VMEM_LIMIT_BYTES = 56 * 1024 * 1024

FFN_ROWS = 512
ATTN_CHAIN_ROWS = 512
RWKV_CHUNK = 64
RWKV_CHUNKS_PER_STEP = 1
RWKV_CUMSUM_ROWS = 256
INV_BLOCK = 16


def _dot(a, b):
    return jnp.dot(a.astype(BF16), b.astype(BF16), preferred_element_type=F32)


def _batched(a, b, ca, cb):
    return lax.dot_general(a.astype(BF16), b.astype(BF16), (((ca,), (cb,)), ((0,), (0,))),
                           preferred_element_type=F32)


def _bdot(a, b):
    return _batched(a, b, 2, 1)


def _bdot_nt(a, b):
    return _batched(a, b, 2, 2)


def _bdot_tn(a, b):
    return _batched(a, b, 1, 1)


def _split2(x):
    hi = x.astype(BF16)
    lo = (x - hi.astype(F32)).astype(BF16)
    return hi, lo


def _split3(x):
    hi = x.astype(BF16)
    r1 = x - hi.astype(F32)
    mid = r1.astype(BF16)
    lo = (r1 - mid.astype(F32)).astype(BF16)
    return hi, mid, lo


def _sigmoid(x):
    return 1.0 / (1.0 + jnp.exp(-x))


def _rms(x, g):
    return x * lax.rsqrt(jnp.mean(x * x, axis=-1, keepdims=True) + NORM_EPS) * g


def _const_spec(shape):
    nd = len(shape)
    return pl.BlockSpec(shape, lambda *_: (0,) * nd, pipeline_mode=pl.Buffered(1))


def _params(semantics):
    return pltpu.CompilerParams(dimension_semantics=semantics, vmem_limit_bytes=VMEM_LIMIT_BYTES)


def _mod_kernel(c_ref, w_ref, b_ref, o_ref):
    c = c_ref[...]
    c_act = c * _sigmoid(c)
    o_ref[...] = _dot(c_act, w_ref[...]) + b_ref[...]


def _mod(c, w_mod, b_mod):
    bsz, d = c.shape
    n = w_mod.shape[1]
    tn = d
    return pl.pallas_call(
        _mod_kernel,
        grid=(n // tn,),
        in_specs=[pl.BlockSpec((bsz, d), lambda j: (0, 0)),
                  pl.BlockSpec((d, tn), lambda j: (0, j)),
                  pl.BlockSpec((1, tn), lambda j: (0, j))],
        out_specs=pl.BlockSpec((bsz, tn), lambda j: (0, j)),
        out_shape=jax.ShapeDtypeStruct((bsz, n), F32),
        compiler_params=_params(("arbitrary",)),
        name="mod",
    )(c, w_mod, b_mod.reshape(1, n))


def _ffn_kernel(*refs, mod_rows, pre_mix, final_norm):
    it = iter(refs)
    h_ref, mod_ref, ng_ref, wg_ref, wu_ref, wd_ref = (next(it) for _ in range(6))
    if pre_mix:
        ya_ref, yb_ref, ang_ref, woa_ref, wob_ref = (next(it) for _ in range(5))
    if final_norm:
        fng_ref = next(it)
    o_ref = next(it)

    h = h_ref[0]
    mod = mod_ref[0]
    row = lambda i: mod[i:i + 1, :]
    if pre_mix:
        ya = _rms(ya_ref[0], ang_ref[...])
        y = _dot(ya, woa_ref[...]) + _dot(yb_ref[0], wob_ref[...])
        h = h + row(mod_rows[3]) * y
    sh, sc, gt = row(mod_rows[0]), row(mod_rows[1]), row(mod_rows[2])
    u = (_rms(h, ng_ref[...]) * (1.0 + sc) + sh).astype(BF16)
    g = jnp.dot(u, wg_ref[...], preferred_element_type=F32)
    up = jnp.dot(u, wu_ref[...], preferred_element_type=F32)
    act = (g * _sigmoid(g) * up).astype(BF16)
    f = jnp.dot(act, wd_ref[...], preferred_element_type=F32)
    h = h + 0.5 * gt * f
    if final_norm:
        h = _rms(h, fng_ref[...])
    o_ref[0] = h


def _ffn(h, mod, ng, wg, wu, wd, *, mod_rows, mix=None, final_g=None):
    bsz, s, d = h.shape
    f = wg.shape[1]
    tm = min(FFN_ROWS, s)
    row_spec = lambda w: pl.BlockSpec((1, tm, w), lambda b, i: (b, i, 0))
    in_specs = [row_spec(d),
                pl.BlockSpec((1, N_MOD, d), lambda b, i: (b, 0, 0)),
                _const_spec((1, d)), _const_spec((d, f)), _const_spec((d, f)), _const_spec((f, d))]
    args = [h, mod, ng.reshape(1, d), wg, wu, wd]
    if mix is not None:
        ya, yb, ang, woa, wob = mix
        wa, wb = ya.shape[-1], yb.shape[-1]
        in_specs += [row_spec(wa), row_spec(wb), _const_spec((1, wa)),
                     _const_spec((wa, d)), _const_spec((wb, d))]
        args += [ya, yb, ang.reshape(1, wa), woa, wob]
    if final_g is not None:
        in_specs.append(_const_spec((1, d)))
        args.append(final_g.reshape(1, d))
    kern = functools.partial(_ffn_kernel, mod_rows=mod_rows, pre_mix=mix is not None,
                             final_norm=final_g is not None)
    return pl.pallas_call(
        kern,
        grid=(bsz, s // tm),
        in_specs=in_specs,
        out_specs=row_spec(d),
        out_shape=jax.ShapeDtypeStruct((bsz, s, d), F32),
        compiler_params=_params(("parallel", "parallel")),
        name="ffn_mix" if mix is not None else "ffn",
    )(*args)


def _inproj_kernel(h_ref, mod_ref, ng_ref, w_ref, pos_ref, cs_ref, qg_ref, kvg_ref, wq_ref, wkv_ref,
                   mix_ref, pr_ref, q_ref, k_ref, v_ref, last_ref, *, n_mla, scale):
    @pl.when(pl.program_id(1) == 0)
    def _():
        last_ref[...] = jnp.zeros_like(last_ref)

    mod = mod_ref[0]
    u = _rms(h_ref[0], ng_ref[...]) * (1.0 + mod[4:5, :]) + mod[3:4, :]
    proj = _dot(u, w_ref[...])
    p_rw = proj[:, n_mla:]
    tm = p_rw.shape[0]
    rowi = lax.broadcasted_iota(jnp.int32, p_rw.shape, 0)
    p_prev = jnp.where(rowi == 0, last_ref[0:1, :], pltpu.roll(p_rw, 1, axis=0))
    pr_ref[0] = p_rw + (p_prev - p_rw) * mix_ref[...]
    last_ref[0:1, :] = p_rw[tm - 1:tm, :]
    p = proj[:, :n_mla]
    nh, nope = MLA_HEADS, MLA_NOPE
    ang = pos_ref[0] * cs_ref[0:1, :]
    lane = lax.broadcasted_iota(jnp.int32, ang.shape, 1)
    low = lane < MLA_ROPE
    cs = jnp.where(low, jnp.cos(ang), cs_ref[1:2, :] * jnp.sin(ang))

    def rope(xpair):
        prod = xpair * cs
        return jnp.where(low, prod + pltpu.roll(prod, MLA_ROPE, axis=1), 0.0)

    q = _dot(_rms(p[:, :Q_LORA], qg_ref[...]), wq_ref[...]) * scale
    kv = _dot(_rms(p[:, Q_LORA:Q_LORA + KV_LORA], kvg_ref[...]), wkv_ref[...])
    k_rope = rope(p[:, Q_LORA + KV_LORA:]).astype(BF16)
    for h in range(nh):
        o = 2 * LANES * h
        q_ref[0, :, o:o + nope] = q[:, nope * h:nope * (h + 1)].astype(BF16)
        q_ref[0, :, o + nope:o + 2 * LANES] = rope(
            q[:, nh * nope + LANES * h:nh * nope + LANES * (h + 1)]).astype(BF16)
        k_ref[0, :, o:o + nope] = kv[:, nope * h:nope * (h + 1)].astype(BF16)
        k_ref[0, :, o + nope:o + 2 * LANES] = k_rope
        v_ref[0, :, o:o + MLA_V] = kv[:, nh * nope + MLA_V * h:nh * nope + MLA_V * (h + 1)].astype(BF16)
        v_ref[0, :, o + MLA_V:o + 2 * LANES] = (lane == 0).astype(BF16)


def _inproj(h, mod, ng, w, n_mla, pos, cs, qg, kvg, wq, wkv, mix):
    bsz, s, d = h.shape
    n = w.shape[1]
    tm = min(FFN_ROWS, s)
    row_spec = lambda wd_: pl.BlockSpec((1, tm, wd_), lambda b, i: (b, i, 0))
    qk_w = MLA_HEADS * 2 * LANES
    v_w = qk_w
    return pl.pallas_call(
        functools.partial(_inproj_kernel, n_mla=n_mla, scale=(MLA_NOPE + MLA_ROPE) ** -0.5 * LOG2_E),
        grid=(bsz, s // tm),
        in_specs=[row_spec(d), pl.BlockSpec((1, N_MOD, d), lambda b, i: (b, 0, 0)),
                  _const_spec((1, d)), _const_spec((d, n)), row_spec(1), _const_spec((2, LANES)),
                  _const_spec((1, Q_LORA)), _const_spec((1, KV_LORA)),
                  _const_spec(wq.shape), _const_spec(wkv.shape), _const_spec((1, n - n_mla))],
        out_specs=[row_spec(n - n_mla), row_spec(qk_w), row_spec(qk_w), row_spec(v_w)],
        out_shape=[jax.ShapeDtypeStruct((bsz, s, n - n_mla), F32),
                   jax.ShapeDtypeStruct((bsz, s, qk_w), BF16),
                   jax.ShapeDtypeStruct((bsz, s, qk_w), BF16),
                   jax.ShapeDtypeStruct((bsz, s, v_w), BF16)],
        scratch_shapes=[pltpu.VMEM((8, n - n_mla), F32)],
        compiler_params=_params(("parallel", "arbitrary")),
        name="inproj",
    )(h, mod, ng.reshape(1, d), w, pos, cs, qg.reshape(1, -1), kvg.reshape(1, -1), wq, wkv,
      mix.reshape(1, -1))


def _flash_kernel(q_ref, k_ref, v_ref, o_ref, *, bq):
    i = pl.program_id(2)
    bk = 2 * bq
    base = pl.multiple_of(i * bk, bk)

    def scores(chain, start, n):
        kb = k_ref[0, pl.ds(start, n), :]
        return lax.dot_general(q_ref[0, chain * bq:(chain + 1) * bq, :], kb, (((1,), (1,)), ((), ())),
                               preferred_element_type=F32)

    def first(s, start, n):
        m = jnp.max(s, axis=-1, keepdims=True)
        p = jnp.exp2((s - m).astype(BF16))
        return m, jnp.dot(p, v_ref[0, pl.ds(start, n), :], preferred_element_type=F32)

    def update(carry, s, start, n):
        m, acc = carry
        m_new = jnp.maximum(m, jnp.max(s, axis=-1, keepdims=True))
        alpha = jnp.exp2(m - m_new)
        p = jnp.exp2((s - m_new).astype(BF16))
        return m_new, alpha * acc + jnp.dot(p, v_ref[0, pl.ds(start, n), :], preferred_element_type=F32)

    visible = (lax.broadcasted_iota(jnp.int32, (bq, bq), 1) // ATTN_CHUNK
               <= lax.broadcasted_iota(jnp.int32, (bq, bq), 0) // ATTN_CHUNK)
    mid = pl.multiple_of(base + bq, bq)
    s_a = jnp.where(visible, scores(0, base, bq), -1e30)
    s_b = jnp.where(visible, scores(1, mid, bq), -1e30)
    s_b0 = scores(1, base, bq)
    chain_a = first(s_a, base, bq)
    chain_b = update(first(s_b, mid, bq), s_b0, base, bq)

    def block(j, carry):
        chain_a, chain_b = carry
        start = pl.multiple_of(j * bk, bk)
        s_a = scores(0, start, bk)
        s_b = scores(1, start, bk)
        return update(chain_a, s_a, start, bk), update(chain_b, s_b, start, bk)

    pair = lambda jj, carry: block(2 * jj + 1, block(2 * jj, carry))
    carry = lax.fori_loop(0, i // 2, pair, (chain_a, chain_b))
    chain_a, chain_b = lax.fori_loop(0, i % 2, lambda _, carry: block(i - 1, carry), carry)
    for chain, (_, acc) in enumerate((chain_a, chain_b)):
        o_ref[0, chain * bq:(chain + 1) * bq, :] = acc[:, :MLA_V] / acc[:, MLA_V:MLA_V + 1]


def _flash(q, k, v):
    bsz, s, _ = q.shape
    bq = min(ATTN_CHAIN_ROWS, s // 2)
    qk_w = 2 * LANES
    return pl.pallas_call(
        functools.partial(_flash_kernel, bq=bq),
        grid=(bsz, MLA_HEADS, s // (2 * bq)),
        in_specs=[pl.BlockSpec((1, 2 * bq, qk_w), lambda b, h, i: (b, i, h)),
                  pl.BlockSpec((1, s, qk_w), lambda b, h, i: (b, 0, h)),
                  pl.BlockSpec((1, s, qk_w), lambda b, h, i: (b, 0, h))],
        out_specs=pl.BlockSpec((1, 2 * bq, MLA_V), lambda b, h, i: (b, i, h)),
        out_shape=jax.ShapeDtypeStruct((bsz, s, MLA_HEADS * MLA_V), F32),
        compiler_params=_params(("parallel", "parallel", "arbitrary")),
        name="flash",
    )(q, k, v)


def _pair_diag(y, pairmask):
    yb = y.astype(BF16)
    return jnp.where(pairmask, jnp.concatenate([yb, yb], axis=1), jnp.zeros((), BF16))


def _unit_lower_inverse(a, eye, blockdiag, pairmask):
    mm = lambda x, yd: _bdot(x, yd)
    ad = jnp.where(blockdiag, a, 0.0)
    ao = a - ad
    ad_d = _pair_diag(ad, pairmask)
    a2 = mm(ad, ad_d)
    a2_d = _pair_diag(a2, pairmask)
    a4 = mm(a2, a2_d)
    a4_d = _pair_diag(a4, pairmask)
    a8 = mm(a4, a4_d)
    td = eye + ad
    td = td + mm(td, a2_d)
    td = td + mm(td, a4_d)
    td = td + mm(td, _pair_diag(a8, pairmask))
    n1 = mm(td, _pair_diag(ao, pairmask))
    n1_d = _pair_diag(n1, pairmask)
    n2 = mm(n1, n1_d)
    x = eye + n1
    x = x + mm(x, _pair_diag(n2, pairmask))
    return mm(x, _pair_diag(td, pairmask))


def _rwkv_kernel(p_ref, w0_ref, a0_ref, kk_ref, ka_ref, rk_ref, lnw_ref, lnb_ref,
                 wl_ref, g2_ref, ones_ref, tri_ref, o_ref, state_ref, y_ref, *, nseq, nchunks):
    t = pl.program_id(0)
    nl = RWKV_CHUNK
    hd = RWKV_HEAD
    c_w = o_ref.shape[-1]
    nheads = c_w // hd

    @pl.when(t == 0)
    def _():
        state_ref[...] = jnp.zeros_like(state_ref)

    p = p_ref[...].reshape(nseq * nchunks * nl, p_ref.shape[-1])

    r = p[:, :c_w]
    k = p[:, c_w:2 * c_w]
    v = p[:, 2 * c_w:3 * c_w]
    xwa = p[:, 3 * c_w:3 * c_w + LANES]
    xg = p[:, 3 * c_w + LANES:]

    def segsum(x, terms):
        halves = []
        for o in range(0, c_w, 2 * LANES):
            xo = x[:, o:o + 2 * LANES]
            pieces = _split2(xo) if terms == 2 else (xo.astype(BF16),)
            halves.append(sum(jnp.dot(pc, ones_ref[...], preferred_element_type=F32) for pc in pieces))
        return jnp.concatenate(halves, axis=1)

    def rsqrt(x):
        return jnp.exp2(-0.5 * jnp.log2(x))

    def sigmoid(x):
        return 0.5 * jnp.tanh(0.5 * x) + 0.5

    lane = lax.broadcasted_iota(jnp.int32, xwa.shape, 1)
    wa = _dot(jnp.where(lane < DECAY_LORA, jnp.tanh(xwa), xwa), wl_ref[...])
    z = w0_ref[...] + wa[:, :c_w]
    lw = -EXP_NEG_HALF * sigmoid(z)
    a = sigmoid(a0_ref[...] + wa[:, c_w:])
    g = _dot(sigmoid(xg), g2_ref[...])
    kk = k * kk_ref[...]
    kk = kk * rsqrt(jnp.maximum(segsum(kk * kk, 1), 1e-24))
    k2 = k * (1.0 + (a - 1.0) * ka_ref[...])
    bb = kk * a

    tri = tri_ref[...]
    grp = tri.shape[0]
    parts = _split3(lw)
    c = jnp.concatenate(
        [sum(jnp.dot(tri, part[o:o + grp], preferred_element_type=F32) for part in parts)
         for o in range(0, lw.shape[0], grp)], axis=0)
    slots = [(b, j) for b in range(nseq) for j in range(nchunks)]
    rows = lambda x, b, j: x[nl * (b * nchunks + j):nl * (b * nchunks + j + 1)]
    e_last = {bj: jnp.exp(rows(c, *bj)[nl - 1:nl, :]) for bj in slots}
    e_neg = jnp.exp(-c)
    e_rem = jnp.concatenate([e_last[bj] * rows(e_neg, *bj) for bj in slots], axis=0)
    a_t = -kk * jnp.exp(c - lw)
    r_t = r * jnp.exp(c)
    b_t = bb * e_neg
    k_t = k2 * e_neg
    b_p = bb * e_rem
    k_p = k2 * e_rem

    pw = 2 * hd
    npairs = nheads // 2

    def pairs(x):
        return jnp.stack([x[:, pw * q:pw * (q + 1)] for q in range(npairs)], axis=0)

    def all_pairs(x):
        return jnp.concatenate([pairs(rows(x, b, j)) for j in range(nchunks) for b in range(nseq)], axis=0)

    ri = lax.broadcasted_iota(jnp.int32, (1, nl, pw), 1)
    ci = lax.broadcasted_iota(jnp.int32, (1, nl, pw), 2) % hd
    strict = ci < ri
    incl = ci <= ri
    eye = (ci == ri).astype(F32)
    blockdiag = (ri // INV_BLOCK) == (ci // INV_BLOCK)
    pairmask = (lax.broadcasted_iota(jnp.int32, (1, pw, pw), 1) // hd
                == lax.broadcasted_iota(jnp.int32, (1, pw, pw), 2) // hd)

    a_p = all_pairs(a_t)
    r_p = all_pairs(r_t)
    ar = jnp.concatenate([a_p, r_p], axis=1)
    g_bk = _bdot_nt(ar, jnp.concatenate([_pair_diag(all_pairs(b_t), pairmask),
                                         _pair_diag(all_pairs(k_t), pairmask)], axis=1))
    gb, gk = g_bk[:, :, :pw], g_bk[:, :, pw:]
    a_rb = jnp.where(incl, gb[:, nl:], 0.0)
    a_ak = jnp.where(strict, gk[:, :nl], 0.0)
    a_rk = jnp.where(incl, gk[:, nl:], 0.0)
    tinv = _unit_lower_inverse(jnp.where(strict, gb[:, :nl], 0.0), eye, blockdiag, pairmask)
    v_p = all_pairs(v)
    v_d = _pair_diag(v_p, pairmask)
    akv = _bdot(a_ak, v_d)
    rkv = _bdot(a_rk, v_d)
    kpv = _bdot_tn(v_p, all_pairs(k_p))
    b_pp = all_pairs(b_p).astype(BF16)
    ta_u0 = _bdot(tinv, jnp.concatenate([_pair_diag(a_p, pairmask), _pair_diag(akv, pairmask)], axis=2))
    u0 = ta_u0[:, :, pw:]
    tar = jnp.concatenate([ta_u0[:, :, :pw], r_p], axis=1).astype(BF16)

    st = state_ref[...]
    per_chunk = nseq * npairs
    for j in range(nchunks):
        sj = slice(per_chunk * j, per_chunk * (j + 1))
        sz = _bdot_nt(tar[sj], st)
        u = sz[:, :nl] + u0[sj]
        y_j = sz[:, nl:] + _bdot(a_rb[sj], _pair_diag(u, pairmask)) + rkv[sj]
        for b in range(nseq):
            r0 = nl * (b * nchunks + j)
            for q in range(npairs):
                y_ref[r0:r0 + nl, pw * q:pw * (q + 1)] = y_j[b * npairs + q]
        decay = jnp.concatenate([pairs(e_last[(b, j)]) for b in range(nseq)], axis=0)
        st = jnp.where(pairmask, st * decay + _bdot_tn(u, b_pp[sj]) + kpv[sj], 0.0)
    state_ref[...] = st

    y = y_ref[...]
    inv_n = 1.0 / hd
    mu = segsum(y, 2) * inv_n
    d = y - mu
    var = segsum(d * d, 1) * inv_n
    yn = d * rsqrt(var + GN_EPS) * lnw_ref[...] + lnb_ref[...]
    bonus = segsum(r * k2 * rk_ref[...], 1) * v
    o_ref[...] = ((yn + bonus) * g).reshape(o_ref.shape)


def _rwkv(p_rw, w0, a0, k_k, k_a, r_k, ln_w, ln_b, wl, g2):
    bsz, s, n_rw = p_rw.shape
    c_w = w0.shape[-1]
    nchunks = min(RWKV_CHUNKS_PER_STEP, s // RWKV_CHUNK)
    nl = RWKV_CHUNK * nchunks
    nheads = c_w // RWKV_HEAD
    seg = jnp.arange(2 * LANES) // RWKV_HEAD
    ones_bd = (seg[:, None] == seg[None, :]).astype(BF16)
    pos = jnp.arange(min(bsz * nl, RWKV_CUMSUM_ROWS))
    tri = ((pos[None, :] <= pos[:, None])
           & (pos[None, :] // RWKV_CHUNK == pos[:, None] // RWKV_CHUNK)).astype(BF16)
    vec = lambda x: x.reshape(1, -1)
    return pl.pallas_call(
        functools.partial(_rwkv_kernel, nseq=bsz, nchunks=nchunks),
        grid=(s // nl,),
        in_specs=[pl.BlockSpec((bsz, nl, n_rw), lambda t: (0, t, 0))] + [_const_spec((1, c_w))] * 7
                 + [_const_spec(wl.shape), _const_spec(g2.shape), _const_spec(ones_bd.shape),
                    _const_spec(tri.shape)],
        out_specs=pl.BlockSpec((bsz, nl, c_w), lambda t: (0, t, 0)),
        out_shape=jax.ShapeDtypeStruct((bsz, s, c_w), F32),
        scratch_shapes=[pltpu.VMEM((bsz * (nheads // 2), 2 * RWKV_HEAD, 2 * RWKV_HEAD), F32),
                        pltpu.VMEM((bsz * nl, c_w), F32)],
        compiler_params=_params(("arbitrary",)),
        name="rwkv",
    )(p_rw, vec(w0), vec(a0), vec(k_k), vec(k_a), vec(r_k), vec(ln_w), vec(ln_b),
      wl, g2, ones_bd, tri)


def _swap_halves(w):
    half = w.shape[-1] // 2
    return jnp.concatenate([w[..., half:], w[..., :half]], axis=-1)


def kernel(x, c, positions, w_mod, b_mod, ffn1_norm_g, ffn1_w_gate, ffn1_w_up, ffn1_w_down, mix_norm_g, w_in, q_norm_g, w_uq, kv_norm_g, w_ukv, attn_out_norm_g, rwkv_shift_mix, rwkv_w0, rwkv_w2, rwkv_a0, rwkv_a2, rwkv_g2, rwkv_k_k, rwkv_k_a, rwkv_r_k, rwkv_ln_w, rwkv_ln_b, w_out, ffn2_norm_g, ffn2_w_gate, ffn2_w_up, ffn2_w_down, final_norm_g):
    depth = w_mod.shape[0]
    bsz, s, d = x.shape
    nh, nope, rope_w, vw = MLA_HEADS, MLA_NOPE, MLA_ROPE, MLA_V
    mla_cols = Q_LORA + KV_LORA + rope_w
    mla_w = MLA_HEADS * MLA_V

    half = rope_w // 2
    inv_freq = ROPE_BASE ** (-jnp.arange(half, dtype=F32) / half)
    sine_sign = jnp.where(jnp.arange(LANES) < 3 * half, -1.0, 1.0).astype(F32)
    cs = jnp.stack([jnp.tile(inv_freq, LANES // half), sine_sign])
    pos = positions.astype(F32)[..., None]

    h = x
    for l in range(depth):
        w_kr = w_in[l][:, Q_LORA + KV_LORA:mla_cols]
        w_in_l = jnp.concatenate([w_in[l][:, :mla_cols], _swap_halves(w_kr), w_in[l][:, mla_cols:]],
                                 axis=1).astype(BF16)
        n_mla = mla_cols + rope_w
        wq = w_uq[l].reshape(Q_LORA, nh, nope + rope_w)
        wq_rope = wq[:, :, nope:]
        wq_l = jnp.concatenate(
            [wq[:, :, :nope].reshape(Q_LORA, nh * nope),
             jnp.concatenate([wq_rope, _swap_halves(wq_rope)], axis=-1).reshape(Q_LORA, nh * 2 * rope_w)],
            axis=1).astype(BF16)
        wkv = w_ukv[l].reshape(KV_LORA, nh, nope + vw)
        wkv_l = jnp.concatenate([wkv[:, :, :nope].reshape(KV_LORA, nh * nope),
                                 wkv[:, :, nope:].reshape(KV_LORA, nh * vw)], axis=1).astype(BF16)
        c_w = rwkv_w0.shape[-1]
        zeros = jnp.zeros((DECAY_LORA, c_w), F32)
        wl = jnp.concatenate([jnp.concatenate([rwkv_w2[l], zeros], axis=1),
                              jnp.concatenate([zeros, rwkv_a2[l]], axis=1)], axis=0).astype(BF16)

        mod = _mod(c, w_mod[l], b_mod[l]).reshape(bsz, N_MOD, d)
        h = _ffn(h, mod, ffn1_norm_g[l], ffn1_w_gate[l].astype(BF16), ffn1_w_up[l].astype(BF16),
                 ffn1_w_down[l].astype(BF16), mod_rows=(0, 1, 2))
        p_rw, q, k, v = _inproj(h, mod, mix_norm_g[l], w_in_l, n_mla, pos, cs, q_norm_g[l], kv_norm_g[l],
                                wq_l, wkv_l, rwkv_shift_mix[l])
        y_a = _flash(q, k, v)
        y_b = _rwkv(p_rw, rwkv_w0[l], rwkv_a0[l], rwkv_k_k[l], rwkv_k_a[l],
                    rwkv_r_k[l].reshape(-1), rwkv_ln_w[l], rwkv_ln_b[l], wl, rwkv_g2[l].astype(BF16))
        mix = (y_a, y_b, attn_out_norm_g[l], w_out[l][:mla_w].astype(BF16), w_out[l][mla_w:].astype(BF16))
        h = _ffn(h, mod, ffn2_norm_g[l], ffn2_w_gate[l].astype(BF16), ffn2_w_up[l].astype(BF16),
                 ffn2_w_down[l].astype(BF16), mod_rows=(6, 7, 8, 5), mix=mix,
                 final_g=final_norm_g if l == depth - 1 else None)
    return h
```

```python
import functools

import jax
import jax.numpy as jnp
from jax import lax
from jax.experimental import pallas as pl
from jax.experimental.pallas import tpu as pltpu

F32 = jnp.float32
BF16 = jnp.bfloat16

MLA_HEADS = 4
MLA_NOPE = 128
MLA_ROPE = 64
MLA_V = 128
Q_LORA = 384
KV_LORA = 256
ROPE_BASE = 10000.0
RWKV_HEAD = 64
DECAY_LORA = 64
AAA_LORA = 64
GATE_LORA = 128
NORM_EPS = 1e-6
GN_EPS = 64e-5
N_MOD = 9
ATTN_CHUNK = 64
LOG2_E = 1.4426950408889634
EXP_NEG_HALF = 0.6065306597126334

LANES = 128
VMEM_LIMIT_BYTES = 56 * 1024 * 1024

FFN_ROWS = 512
ATTN_CHAIN_ROWS = 1024
RWKV_CHUNK = 64
RWKV_CHUNKS_PER_STEP = 1
RWKV_CUMSUM_ROWS = 256
INV_BLOCK = 16


def _dot(a, b):
    return jnp.dot(a.astype(BF16), b.astype(BF16), preferred_element_type=F32)


def _batched(a, b, ca, cb):
    return lax.dot_general(a.astype(BF16), b.astype(BF16), (((ca,), (cb,)), ((0,), (0,))),
                           preferred_element_type=F32)


def _bdot(a, b):
    return _batched(a, b, 2, 1)


def _bdot_nt(a, b):
    return _batched(a, b, 2, 2)


def _bdot_tn(a, b):
    return _batched(a, b, 1, 1)


def _split2(x):
    hi = x.astype(BF16)
    lo = (x - hi.astype(F32)).astype(BF16)
    return hi, lo


def _split3(x):
    hi = x.astype(BF16)
    r1 = x - hi.astype(F32)
    mid = r1.astype(BF16)
    lo = (r1 - mid.astype(F32)).astype(BF16)
    return hi, mid, lo


def _sigmoid(x):
    return 1.0 / (1.0 + jnp.exp(-x))


def _rms(x, g):
    return x * lax.rsqrt(jnp.mean(x * x, axis=-1, keepdims=True) + NORM_EPS) * g


def _const_spec(shape):
    nd = len(shape)
    return pl.BlockSpec(shape, lambda *_: (0,) * nd, pipeline_mode=pl.Buffered(1))


def _params(semantics):
    return pltpu.CompilerParams(dimension_semantics=semantics, vmem_limit_bytes=VMEM_LIMIT_BYTES)


def _mod_kernel(c_ref, w_ref, b_ref, o_ref):
    c = c_ref[...]
    c_act = c * _sigmoid(c)
    o_ref[...] = _dot(c_act, w_ref[...]) + b_ref[...]


def _mod(c, w_mod, b_mod):
    bsz, d = c.shape
    n = w_mod.shape[1]
    tn = d
    return pl.pallas_call(
        _mod_kernel,
        grid=(n // tn,),
        in_specs=[pl.BlockSpec((bsz, d), lambda j: (0, 0)),
                  pl.BlockSpec((d, tn), lambda j: (0, j)),
                  pl.BlockSpec((1, tn), lambda j: (0, j))],
        out_specs=pl.BlockSpec((bsz, tn), lambda j: (0, j)),
        out_shape=jax.ShapeDtypeStruct((bsz, n), F32),
        compiler_params=_params(("arbitrary",)),
        name="mod",
    )(c, w_mod, b_mod.reshape(1, n))


def _ffn_kernel(*refs, mod_rows, pre_mix, final_norm):
    it = iter(refs)
    h_ref, mod_ref, ng_ref, wg_ref, wu_ref, wd_ref = (next(it) for _ in range(6))
    if pre_mix:
        ya_ref, yb_ref, ang_ref, woa_ref, wob_ref = (next(it) for _ in range(5))
    if final_norm:
        fng_ref = next(it)
    o_ref = next(it)

    h = h_ref[0]
    mod = mod_ref[0]
    row = lambda i: mod[i:i + 1, :]
    if pre_mix:
        ya = _rms(ya_ref[0], ang_ref[...])
        y = _dot(ya, woa_ref[...]) + _dot(yb_ref[0], wob_ref[...])
        h = h + row(mod_rows[3]) * y
    sh, sc, gt = row(mod_rows[0]), row(mod_rows[1]), row(mod_rows[2])
    u = (_rms(h, ng_ref[...]) * (1.0 + sc) + sh).astype(BF16)
    g = jnp.dot(u, wg_ref[...], preferred_element_type=F32)
    up = jnp.dot(u, wu_ref[...], preferred_element_type=F32)
    act = (g * _sigmoid(g) * up).astype(BF16)
    f = jnp.dot(act, wd_ref[...], preferred_element_type=F32)
    h = h + 0.5 * gt * f
    if final_norm:
        h = _rms(h, fng_ref[...])
    o_ref[0] = h


def _ffn(h, mod, ng, wg, wu, wd, *, mod_rows, mix=None, final_g=None):
    bsz, s, d = h.shape
    f = wg.shape[1]
    tm = min(FFN_ROWS, s)
    row_spec = lambda w: pl.BlockSpec((1, tm, w), lambda b, i: (b, i, 0))
    in_specs = [row_spec(d),
                pl.BlockSpec((1, N_MOD, d), lambda b, i: (b, 0, 0)),
                _const_spec((1, d)), _const_spec((d, f)), _const_spec((d, f)), _const_spec((f, d))]
    args = [h, mod, ng.reshape(1, d), wg, wu, wd]
    if mix is not None:
        ya, yb, ang, woa, wob = mix
        wa, wb = ya.shape[-1], yb.shape[-1]
        in_specs += [row_spec(wa), row_spec(wb), _const_spec((1, wa)),
                     _const_spec((wa, d)), _const_spec((wb, d))]
        args += [ya, yb, ang.reshape(1, wa), woa, wob]
    if final_g is not None:
        in_specs.append(_const_spec((1, d)))
        args.append(final_g.reshape(1, d))
    kern = functools.partial(_ffn_kernel, mod_rows=mod_rows, pre_mix=mix is not None,
                             final_norm=final_g is not None)
    return pl.pallas_call(
        kern,
        grid=(bsz, s // tm),
        in_specs=in_specs,
        out_specs=row_spec(d),
        out_shape=jax.ShapeDtypeStruct((bsz, s, d), F32),
        compiler_params=_params(("parallel", "parallel")),
        name="ffn_mix" if mix is not None else "ffn",
    )(*args)


def _inproj_kernel(h_ref, mod_ref, ng_ref, w_ref, pos_ref, cs_ref, qg_ref, kvg_ref, wq_ref, wkv_ref,
                   mix_ref, pr_ref, q_ref, k_ref, v_ref, last_ref, *, n_mla, scale):
    @pl.when(pl.program_id(1) == 0)
    def _():
        last_ref[...] = jnp.zeros_like(last_ref)

    mod = mod_ref[0]
    u = _rms(h_ref[0], ng_ref[...]) * (1.0 + mod[4:5, :]) + mod[3:4, :]
    proj = _dot(u, w_ref[...])
    p_rw = proj[:, n_mla:]
    tm = p_rw.shape[0]
    rowi = lax.broadcasted_iota(jnp.int32, p_rw.shape, 0)
    p_prev = jnp.where(rowi == 0, last_ref[0:1, :], pltpu.roll(p_rw, 1, axis=0))
    pr_ref[0] = p_rw + (p_prev - p_rw) * mix_ref[...]
    last_ref[0:1, :] = p_rw[tm - 1:tm, :]
    p = proj[:, :n_mla]
    nh, nope = MLA_HEADS, MLA_NOPE
    ang = pos_ref[0] * cs_ref[0:1, :]
    lane = lax.broadcasted_iota(jnp.int32, ang.shape, 1)
    low = lane < MLA_ROPE
    cs = jnp.where(low, jnp.cos(ang), cs_ref[1:2, :] * jnp.sin(ang))

    def rope(xpair):
        prod = xpair * cs
        return jnp.where(low, prod + pltpu.roll(prod, MLA_ROPE, axis=1), 0.0)

    q = _dot(_rms(p[:, :Q_LORA], qg_ref[...]), wq_ref[...]) * scale
    kv = _dot(_rms(p[:, Q_LORA:Q_LORA + KV_LORA], kvg_ref[...]), wkv_ref[...])
    k_rope = rope(p[:, Q_LORA + KV_LORA:]).astype(BF16)
    for h in range(nh):
        o = 2 * LANES * h
        q_ref[0, :, o:o + nope] = q[:, nope * h:nope * (h + 1)].astype(BF16)
        q_ref[0, :, o + nope:o + 2 * LANES] = rope(
            q[:, nh * nope + LANES * h:nh * nope + LANES * (h + 1)]).astype(BF16)
        k_ref[0, :, o:o + nope] = kv[:, nope * h:nope * (h + 1)].astype(BF16)
        k_ref[0, :, o + nope:o + 2 * LANES] = k_rope
        v_ref[0, :, o:o + MLA_V] = kv[:, nh * nope + MLA_V * h:nh * nope + MLA_V * (h + 1)].astype(BF16)
        v_ref[0, :, o + MLA_V:o + 2 * LANES] = (lane == 0).astype(BF16)


def _inproj(h, mod, ng, w, n_mla, pos, cs, qg, kvg, wq, wkv, mix):
    bsz, s, d = h.shape
    n = w.shape[1]
    tm = min(FFN_ROWS, s)
    row_spec = lambda wd_: pl.BlockSpec((1, tm, wd_), lambda b, i: (b, i, 0))
    qk_w = MLA_HEADS * 2 * LANES
    v_w = qk_w
    return pl.pallas_call(
        functools.partial(_inproj_kernel, n_mla=n_mla, scale=(MLA_NOPE + MLA_ROPE) ** -0.5 * LOG2_E),
        grid=(bsz, s // tm),
        in_specs=[row_spec(d), pl.BlockSpec((1, N_MOD, d), lambda b, i: (b, 0, 0)),
                  _const_spec((1, d)), _const_spec((d, n)), row_spec(1), _const_spec((2, LANES)),
                  _const_spec((1, Q_LORA)), _const_spec((1, KV_LORA)),
                  _const_spec(wq.shape), _const_spec(wkv.shape), _const_spec((1, n - n_mla))],
        out_specs=[row_spec(n - n_mla), row_spec(qk_w), row_spec(qk_w), row_spec(v_w)],
        out_shape=[jax.ShapeDtypeStruct((bsz, s, n - n_mla), F32),
                   jax.ShapeDtypeStruct((bsz, s, qk_w), BF16),
                   jax.ShapeDtypeStruct((bsz, s, qk_w), BF16),
                   jax.ShapeDtypeStruct((bsz, s, v_w), BF16)],
        scratch_shapes=[pltpu.VMEM((8, n - n_mla), F32)],
        compiler_params=_params(("parallel", "arbitrary")),
        name="inproj",
    )(h, mod, ng.reshape(1, d), w, pos, cs, qg.reshape(1, -1), kvg.reshape(1, -1), wq, wkv,
      mix.reshape(1, -1))


def _flash_kernel(q_ref, k_ref, v_ref, o_ref, *, bq):
    i = pl.program_id(2)
    bk = 2 * bq
    base = pl.multiple_of(i * bk, bk)

    def scores(chain, start, n):
        kb = k_ref[0, pl.ds(start, n), :]
        return lax.dot_general(q_ref[0, chain * bq:(chain + 1) * bq, :], kb, (((1,), (1,)), ((), ())),
                               preferred_element_type=F32)

    def first(s, start, n):
        m = jnp.max(s, axis=-1, keepdims=True)
        p = jnp.exp2((s - m).astype(BF16))
        return m, jnp.dot(p, v_ref[0, pl.ds(start, n), :], preferred_element_type=F32)

    def update(carry, s, start, n):
        m, acc = carry
        m_new = jnp.maximum(m, jnp.max(s, axis=-1, keepdims=True))
        alpha = jnp.exp2(m - m_new)
        p = jnp.exp2((s - m_new).astype(BF16))
        return m_new, alpha * acc + jnp.dot(p, v_ref[0, pl.ds(start, n), :], preferred_element_type=F32)

    visible = (lax.broadcasted_iota(jnp.int32, (bq, bq), 1) // ATTN_CHUNK
               <= lax.broadcasted_iota(jnp.int32, (bq, bq), 0) // ATTN_CHUNK)
    mid = pl.multiple_of(base + bq, bq)
    s_a = jnp.where(visible, scores(0, base, bq), -1e30)
    s_b = jnp.where(visible, scores(1, mid, bq), -1e30)
    s_b0 = scores(1, base, bq)
    chain_a = first(s_a, base, bq)
    chain_b = update(first(s_b, mid, bq), s_b0, base, bq)

    def block(j, carry):
        chain_a, chain_b = carry
        start = pl.multiple_of(j * bk, bk)
        s_a = scores(0, start, bk)
        s_b = scores(1, start, bk)
        return update(chain_a, s_a, start, bk), update(chain_b, s_b, start, bk)

    pair = lambda jj, carry: block(2 * jj + 1, block(2 * jj, carry))
    carry = lax.fori_loop(0, i // 2, pair, (chain_a, chain_b))
    chain_a, chain_b = lax.fori_loop(0, i % 2, lambda _, carry: block(i - 1, carry), carry)
    for chain, (_, acc) in enumerate((chain_a, chain_b)):
        o_ref[0, chain * bq:(chain + 1) * bq, :] = acc[:, :MLA_V] / acc[:, MLA_V:MLA_V + 1]


def _flash(q, k, v):
    bsz, s, _ = q.shape
    bq = min(ATTN_CHAIN_ROWS, s // 2)
    qk_w = 2 * LANES
    return pl.pallas_call(
        functools.partial(_flash_kernel, bq=bq),
        grid=(bsz, MLA_HEADS, s // (2 * bq)),
        in_specs=[pl.BlockSpec((1, 2 * bq, qk_w), lambda b, h, i: (b, i, h)),
                  pl.BlockSpec((1, s, qk_w), lambda b, h, i: (b, 0, h)),
                  pl.BlockSpec((1, s, qk_w), lambda b, h, i: (b, 0, h))],
        out_specs=pl.BlockSpec((1, 2 * bq, MLA_V), lambda b, h, i: (b, i, h)),
        out_shape=jax.ShapeDtypeStruct((bsz, s, MLA_HEADS * MLA_V), F32),
        compiler_params=_params(("parallel", "parallel", "arbitrary")),
        name="flash",
    )(q, k, v)


def _pair_diag(y, pairmask):
    yb = y.astype(BF16)
    return jnp.where(pairmask, jnp.concatenate([yb, yb], axis=1), jnp.zeros((), BF16))


def _unit_lower_inverse(a, eye, blockdiag, pairmask):
    mm = lambda x, yd: _bdot(x, yd)
    ad = jnp.where(blockdiag, a, 0.0)
    ao = a - ad
    ad_d = _pair_diag(ad, pairmask)
    a2 = mm(ad, ad_d)
    a2_d = _pair_diag(a2, pairmask)
    a4 = mm(a2, a2_d)
    a4_d = _pair_diag(a4, pairmask)
    a8 = mm(a4, a4_d)
    td = eye + ad
    td = td + mm(td, a2_d)
    td = td + mm(td, a4_d)
    td = td + mm(td, _pair_diag(a8, pairmask))
    n1 = mm(td, _pair_diag(ao, pairmask))
    n1_d = _pair_diag(n1, pairmask)
    n2 = mm(n1, n1_d)
    x = eye + n1
    x = x + mm(x, _pair_diag(n2, pairmask))
    return mm(x, _pair_diag(td, pairmask))


def _rwkv_kernel(p_ref, w0_ref, a0_ref, kk_ref, ka_ref, rk_ref, lnw_ref, lnb_ref,
                 wl_ref, g2_ref, ones_ref, tri_ref, o_ref, state_ref, y_ref, *, nseq, nchunks):
    t = pl.program_id(0)
    nl = RWKV_CHUNK
    hd = RWKV_HEAD
    c_w = o_ref.shape[-1]
    nheads = c_w // hd

    @pl.when(t == 0)
    def _():
        state_ref[...] = jnp.zeros_like(state_ref)

    p = p_ref[...].reshape(nseq * nchunks * nl, p_ref.shape[-1])

    r = p[:, :c_w]
    k = p[:, c_w:2 * c_w]
    v = p[:, 2 * c_w:3 * c_w]
    xwa = p[:, 3 * c_w:3 * c_w + LANES]
    xg = p[:, 3 * c_w + LANES:]

    def segsum(x, terms):
        halves = []
        for o in range(0, c_w, 2 * LANES):
            xo = x[:, o:o + 2 * LANES]
            pieces = _split2(xo) if terms == 2 else (xo.astype(BF16),)
            halves.append(sum(jnp.dot(pc, ones_ref[...], preferred_element_type=F32) for pc in pieces))
        return jnp.concatenate(halves, axis=1)

    def rsqrt(x):
        return jnp.exp2(-0.5 * jnp.log2(x))

    def sigmoid(x):
        return 0.5 * jnp.tanh(0.5 * x) + 0.5

    lane = lax.broadcasted_iota(jnp.int32, xwa.shape, 1)
    wa = _dot(jnp.where(lane < DECAY_LORA, jnp.tanh(xwa), xwa), wl_ref[...])
    z = w0_ref[...] + wa[:, :c_w]
    lw = -EXP_NEG_HALF * sigmoid(z)
    a = sigmoid(a0_ref[...] + wa[:, c_w:])
    g = _dot(sigmoid(xg), g2_ref[...])
    kk = k * kk_ref[...]
    kk = kk * rsqrt(jnp.maximum(segsum(kk * kk, 1), 1e-24))
    k2 = k * (1.0 + (a - 1.0) * ka_ref[...])
    bb = kk * a

    tri = tri_ref[...]
    grp = tri.shape[0]
    parts = _split3(lw)
    c = jnp.concatenate(
        [sum(jnp.dot(tri, part[o:o + grp], preferred_element_type=F32) for part in parts)
         for o in range(0, lw.shape[0], grp)], axis=0)
    slots = [(b, j) for b in range(nseq) for j in range(nchunks)]
    rows = lambda x, b, j: x[nl * (b * nchunks + j):nl * (b * nchunks + j + 1)]
    e_last = {bj: jnp.exp(rows(c, *bj)[nl - 1:nl, :]) for bj in slots}
    e_neg = jnp.exp(-c)
    e_rem = jnp.concatenate([e_last[bj] * rows(e_neg, *bj) for bj in slots], axis=0)
    a_t = -kk * jnp.exp(c - lw)
    r_t = r * jnp.exp(c)
    b_t = bb * e_neg
    k_t = k2 * e_neg
    b_p = bb * e_rem
    k_p = k2 * e_rem

    pw = 2 * hd
    npairs = nheads // 2

    def pairs(x):
        return jnp.stack([x[:, pw * q:pw * (q + 1)] for q in range(npairs)], axis=0)

    def all_pairs(x):
        return jnp.concatenate([pairs(rows(x, b, j)) for j in range(nchunks) for b in range(nseq)], axis=0)

    ri = lax.broadcasted_iota(jnp.int32, (1, nl, pw), 1)
    ci = lax.broadcasted_iota(jnp.int32, (1, nl, pw), 2) % hd
    strict = ci < ri
    incl = ci <= ri
    eye = (ci == ri).astype(F32)
    blockdiag = (ri // INV_BLOCK) == (ci // INV_BLOCK)
    pairmask = (lax.broadcasted_iota(jnp.int32, (1, pw, pw), 1) // hd
                == lax.broadcasted_iota(jnp.int32, (1, pw, pw), 2) // hd)

    a_p = all_pairs(a_t)
    r_p = all_pairs(r_t)
    ar = jnp.concatenate([a_p, r_p], axis=1)
    g_bk = _bdot_nt(ar, jnp.concatenate([_pair_diag(all_pairs(b_t), pairmask),
                                         _pair_diag(all_pairs(k_t), pairmask)], axis=1))
    gb, gk = g_bk[:, :, :pw], g_bk[:, :, pw:]
    a_rb = jnp.where(incl, gb[:, nl:], 0.0)
    a_ak = jnp.where(strict, gk[:, :nl], 0.0)
    a_rk = jnp.where(incl, gk[:, nl:], 0.0)
    tinv = _unit_lower_inverse(jnp.where(strict, gb[:, :nl], 0.0), eye, blockdiag, pairmask)
    v_p = all_pairs(v)
    v_d = _pair_diag(v_p, pairmask)
    akv = _bdot(a_ak, v_d)
    rkv = _bdot(a_rk, v_d)
    kpv = _bdot_tn(v_p, all_pairs(k_p))
    b_pp = all_pairs(b_p).astype(BF16)
    ta_u0 = _bdot(tinv, jnp.concatenate([_pair_diag(a_p, pairmask), _pair_diag(akv, pairmask)], axis=2))
    u0 = ta_u0[:, :, pw:]
    tar = jnp.concatenate([ta_u0[:, :, :pw], r_p], axis=1).astype(BF16)

    st = state_ref[...]
    per_chunk = nseq * npairs
    for j in range(nchunks):
        sj = slice(per_chunk * j, per_chunk * (j + 1))
        sz = _bdot_nt(tar[sj], st)
        u = sz[:, :nl] + u0[sj]
        y_j = sz[:, nl:] + _bdot(a_rb[sj], _pair_diag(u, pairmask)) + rkv[sj]
        for b in range(nseq):
            r0 = nl * (b * nchunks + j)
            for q in range(npairs):
                y_ref[r0:r0 + nl, pw * q:pw * (q + 1)] = y_j[b * npairs + q]
        decay = jnp.concatenate([pairs(e_last[(b, j)]) for b in range(nseq)], axis=0)
        st = jnp.where(pairmask, st * decay + _bdot_tn(u, b_pp[sj]) + kpv[sj], 0.0)
    state_ref[...] = st

    y = y_ref[...]
    inv_n = 1.0 / hd
    mu = segsum(y, 2) * inv_n
    d = y - mu
    var = segsum(d * d, 1) * inv_n
    yn = d * rsqrt(var + GN_EPS) * lnw_ref[...] + lnb_ref[...]
    bonus = segsum(r * k2 * rk_ref[...], 1) * v
    o_ref[...] = ((yn + bonus) * g).reshape(o_ref.shape)


def _rwkv(p_rw, w0, a0, k_k, k_a, r_k, ln_w, ln_b, wl, g2):
    bsz, s, n_rw = p_rw.shape
    c_w = w0.shape[-1]
    nchunks = min(RWKV_CHUNKS_PER_STEP, s // RWKV_CHUNK)
    nl = RWKV_CHUNK * nchunks
    nheads = c_w // RWKV_HEAD
    seg = jnp.arange(2 * LANES) // RWKV_HEAD
    ones_bd = (seg[:, None] == seg[None, :]).astype(BF16)
    pos = jnp.arange(min(bsz * nl, RWKV_CUMSUM_ROWS))
    tri = ((pos[None, :] <= pos[:, None])
           & (pos[None, :] // RWKV_CHUNK == pos[:, None] // RWKV_CHUNK)).astype(BF16)
    vec = lambda x: x.reshape(1, -1)
    return pl.pallas_call(
        functools.partial(_rwkv_kernel, nseq=bsz, nchunks=nchunks),
        grid=(s // nl,),
        in_specs=[pl.BlockSpec((bsz, nl, n_rw), lambda t: (0, t, 0))] + [_const_spec((1, c_w))] * 7
                 + [_const_spec(wl.shape), _const_spec(g2.shape), _const_spec(ones_bd.shape),
                    _const_spec(tri.shape)],
        out_specs=pl.BlockSpec((bsz, nl, c_w), lambda t: (0, t, 0)),
        out_shape=jax.ShapeDtypeStruct((bsz, s, c_w), F32),
        scratch_shapes=[pltpu.VMEM((bsz * (nheads // 2), 2 * RWKV_HEAD, 2 * RWKV_HEAD), F32),
                        pltpu.VMEM((bsz * nl, c_w), F32)],
        compiler_params=_params(("arbitrary",)),
        name="rwkv",
    )(p_rw, vec(w0), vec(a0), vec(k_k), vec(k_a), vec(r_k), vec(ln_w), vec(ln_b),
      wl, g2, ones_bd, tri)


def _swap_halves(w):
    half = w.shape[-1] // 2
    return jnp.concatenate([w[..., half:], w[..., :half]], axis=-1)


def kernel(x, c, positions, w_mod, b_mod, ffn1_norm_g, ffn1_w_gate, ffn1_w_up, ffn1_w_down, mix_norm_g, w_in, q_norm_g, w_uq, kv_norm_g, w_ukv, attn_out_norm_g, rwkv_shift_mix, rwkv_w0, rwkv_w2, rwkv_a0, rwkv_a2, rwkv_g2, rwkv_k_k, rwkv_k_a, rwkv_r_k, rwkv_ln_w, rwkv_ln_b, w_out, ffn2_norm_g, ffn2_w_gate, ffn2_w_up, ffn2_w_down, final_norm_g):
    depth = w_mod.shape[0]
    bsz, s, d = x.shape
    nh, nope, rope_w, vw = MLA_HEADS, MLA_NOPE, MLA_ROPE, MLA_V
    mla_cols = Q_LORA + KV_LORA + rope_w
    mla_w = MLA_HEADS * MLA_V

    half = rope_w // 2
    inv_freq = ROPE_BASE ** (-jnp.arange(half, dtype=F32) / half)
    sine_sign = jnp.where(jnp.arange(LANES) < 3 * half, -1.0, 1.0).astype(F32)
    cs = jnp.stack([jnp.tile(inv_freq, LANES // half), sine_sign])
    pos = positions.astype(F32)[..., None]

    h = x
    for l in range(depth):
        w_kr = w_in[l][:, Q_LORA + KV_LORA:mla_cols]
        w_in_l = jnp.concatenate([w_in[l][:, :mla_cols], _swap_halves(w_kr), w_in[l][:, mla_cols:]],
                                 axis=1).astype(BF16)
        n_mla = mla_cols + rope_w
        wq = w_uq[l].reshape(Q_LORA, nh, nope + rope_w)
        wq_rope = wq[:, :, nope:]
        wq_l = jnp.concatenate(
            [wq[:, :, :nope].reshape(Q_LORA, nh * nope),
             jnp.concatenate([wq_rope, _swap_halves(wq_rope)], axis=-1).reshape(Q_LORA, nh * 2 * rope_w)],
            axis=1).astype(BF16)
        wkv = w_ukv[l].reshape(KV_LORA, nh, nope + vw)
        wkv_l = jnp.concatenate([wkv[:, :, :nope].reshape(KV_LORA, nh * nope),
                                 wkv[:, :, nope:].reshape(KV_LORA, nh * vw)], axis=1).astype(BF16)
        c_w = rwkv_w0.shape[-1]
        zeros = jnp.zeros((DECAY_LORA, c_w), F32)
        wl = jnp.concatenate([jnp.concatenate([rwkv_w2[l], zeros], axis=1),
                              jnp.concatenate([zeros, rwkv_a2[l]], axis=1)], axis=0).astype(BF16)

        mod = _mod(c, w_mod[l], b_mod[l]).reshape(bsz, N_MOD, d)
        h = _ffn(h, mod, ffn1_norm_g[l], ffn1_w_gate[l].astype(BF16), ffn1_w_up[l].astype(BF16),
                 ffn1_w_down[l].astype(BF16), mod_rows=(0, 1, 2))
        p_rw, q, k, v = _inproj(h, mod, mix_norm_g[l], w_in_l, n_mla, pos, cs, q_norm_g[l], kv_norm_g[l],
                                wq_l, wkv_l, rwkv_shift_mix[l])
        y_a = _flash(q, k, v)
        y_b = _rwkv(p_rw, rwkv_w0[l], rwkv_a0[l], rwkv_k_k[l], rwkv_k_a[l],
                    rwkv_r_k[l].reshape(-1), rwkv_ln_w[l], rwkv_ln_b[l], wl, rwkv_g2[l].astype(BF16))
        mix = (y_a, y_b, attn_out_norm_g[l], w_out[l][:mla_w].astype(BF16), w_out[l][mla_w:].astype(BF16))
        h = _ffn(h, mod, ffn2_norm_g[l], ffn2_w_gate[l].astype(BF16), ffn2_w_up[l].astype(BF16),
                 ffn2_w_down[l].astype(BF16), mod_rows=(6, 7, 8, 5), mix=mix,
                 final_g=final_norm_g if l == depth - 1 else None)
    return h
```

```python
import functools
import math

import jax
import jax.numpy as jnp
from jax import lax
from jax.experimental import pallas as pl
from jax.experimental.pallas import tpu as pltpu

F32 = jnp.float32
BF16 = jnp.bfloat16

MLA_HEADS = 4
MLA_NOPE = 128
MLA_ROPE = 64
MLA_V = 128
Q_LORA = 384
KV_LORA = 256
ROPE_BASE = 10000.0
RWKV_HEAD = 64
DECAY_LORA = 64
AAA_LORA = 64
GATE_LORA = 128
NORM_EPS = 1e-6
GN_EPS = 64e-5
N_MOD = 9
ATTN_CHUNK = 64
LOG2_E = 1.4426950408889634
EXP_NEG_HALF = 0.6065306597126334

LANES = 128
VMEM_LIMIT_BYTES = 56 * 1024 * 1024

FFN_ROWS = 512
ATTN_CHAIN_ROWS = 1024
RWKV_CHUNK = 64
RWKV_CHUNKS_PER_STEP = 2
RWKV_CUMSUM_ROWS = 256
INV_BLOCK = 16


def _dot(a, b):
    return jnp.dot(a.astype(BF16), b.astype(BF16), preferred_element_type=F32)


def _batched(a, b, ca, cb):
    return lax.dot_general(a.astype(BF16), b.astype(BF16), (((ca,), (cb,)), ((0,), (0,))),
                           preferred_element_type=F32)


def _bdot(a, b):
    return _batched(a, b, 2, 1)


def _bdot_nt(a, b):
    return _batched(a, b, 2, 2)


def _bdot_tn(a, b):
    return _batched(a, b, 1, 1)


def _split2(x):
    hi = x.astype(BF16)
    lo = (x - hi.astype(F32)).astype(BF16)
    return hi, lo


def _split3(x):
    hi = x.astype(BF16)
    r1 = x - hi.astype(F32)
    mid = r1.astype(BF16)
    lo = (r1 - mid.astype(F32)).astype(BF16)
    return hi, mid, lo


def _sigmoid(x):
    return 1.0 / (1.0 + jnp.exp(-x))


def _rms(x, g):
    return x * lax.rsqrt(jnp.mean(x * x, axis=-1, keepdims=True) + NORM_EPS) * g


def _const_spec(shape):
    nd = len(shape)
    return pl.BlockSpec(shape, lambda *_: (0,) * nd, pipeline_mode=pl.Buffered(1))


def _params(semantics):
    return pltpu.CompilerParams(dimension_semantics=semantics, vmem_limit_bytes=VMEM_LIMIT_BYTES)


def _mod_kernel(c_ref, w_ref, b_ref, o_ref):
    c = c_ref[...]
    c_act = c * _sigmoid(c)
    o_ref[...] = _dot(c_act, w_ref[...]) + b_ref[...]


def _mod(c, w_mod, b_mod):
    bsz, d = c.shape
    n = w_mod.shape[1]
    tn = d
    return pl.pallas_call(
        _mod_kernel,
        grid=(n // tn,),
        in_specs=[pl.BlockSpec((bsz, d), lambda j: (0, 0)),
                  pl.BlockSpec((d, tn), lambda j: (0, j)),
                  pl.BlockSpec((1, tn), lambda j: (0, j))],
        out_specs=pl.BlockSpec((bsz, tn), lambda j: (0, j)),
        out_shape=jax.ShapeDtypeStruct((bsz, n), F32),
        compiler_params=_params(("arbitrary",)),
        name="mod",
    )(c, w_mod, b_mod.reshape(1, n))


def _ffn_kernel(*refs, mod_rows, pre_mix, final_norm):
    it = iter(refs)
    h_ref, mod_ref, ng_ref, wg_ref, wu_ref, wd_ref = (next(it) for _ in range(6))
    if pre_mix:
        ya_ref, yb_ref, ang_ref, woa_ref, wob_ref = (next(it) for _ in range(5))
    if final_norm:
        fng_ref = next(it)
    o_ref = next(it)

    h = h_ref[0]
    mod = mod_ref[0]
    row = lambda i: mod[i:i + 1, :]
    if pre_mix:
        ya = _rms(ya_ref[0], ang_ref[...])
        y = _dot(ya, woa_ref[...]) + _dot(yb_ref[0], wob_ref[...])
        h = h + row(mod_rows[3]) * y
    sh, sc, gt = row(mod_rows[0]), row(mod_rows[1]), row(mod_rows[2])
    u = (_rms(h, ng_ref[...]) * (1.0 + sc) + sh).astype(BF16)
    g = jnp.dot(u, wg_ref[...], preferred_element_type=F32)
    up = jnp.dot(u, wu_ref[...], preferred_element_type=F32)
    act = (g * _sigmoid(g) * up).astype(BF16)
    f = jnp.dot(act, wd_ref[...], preferred_element_type=F32)
    h = h + 0.5 * gt * f
    if final_norm:
        h = _rms(h, fng_ref[...])
    o_ref[0] = h


def _ffn(h, mod, ng, wg, wu, wd, *, mod_rows, mix=None, final_g=None):
    bsz, s, d = h.shape
    f = wg.shape[1]
    tm = min(FFN_ROWS, s)
    row_spec = lambda w: pl.BlockSpec((1, tm, w), lambda b, i: (b, i, 0))
    in_specs = [row_spec(d),
                pl.BlockSpec((1, N_MOD, d), lambda b, i: (b, 0, 0)),
                _const_spec((1, d)), _const_spec((d, f)), _const_spec((d, f)), _const_spec((f, d))]
    args = [h, mod, ng.reshape(1, d), wg, wu, wd]
    if mix is not None:
        ya, yb, ang, woa, wob = mix
        wa, wb = ya.shape[-1], yb.shape[-1]
        in_specs += [row_spec(wa), row_spec(wb), _const_spec((1, wa)),
                     _const_spec((wa, d)), _const_spec((wb, d))]
        args += [ya, yb, ang.reshape(1, wa), woa, wob]
    if final_g is not None:
        in_specs.append(_const_spec((1, d)))
        args.append(final_g.reshape(1, d))
    kern = functools.partial(_ffn_kernel, mod_rows=mod_rows, pre_mix=mix is not None,
                             final_norm=final_g is not None)
    return pl.pallas_call(
        kern,
        grid=(bsz, s // tm),
        in_specs=in_specs,
        out_specs=row_spec(d),
        out_shape=jax.ShapeDtypeStruct((bsz, s, d), F32),
        compiler_params=_params(("parallel", "parallel")),
        name="ffn_mix" if mix is not None else "ffn",
    )(*args)


def _inproj_kernel(h_ref, mod_ref, ng_ref, w_ref, pos_ref, cs_ref, qg_ref, kvg_ref, wq_ref, wkv_ref,
                   mix_ref, pr_ref, q_ref, k_ref, v_ref, last_ref, *, n_mla, scale):
    @pl.when(pl.program_id(1) == 0)
    def _():
        last_ref[...] = jnp.zeros_like(last_ref)

    mod = mod_ref[0]
    u = _rms(h_ref[0], ng_ref[...]) * (1.0 + mod[4:5, :]) + mod[3:4, :]
    proj = _dot(u, w_ref[...])
    p_rw = proj[:, n_mla:]
    tm = p_rw.shape[0]
    rowi = lax.broadcasted_iota(jnp.int32, p_rw.shape, 0)
    p_prev = jnp.where(rowi == 0, last_ref[0:1, :], pltpu.roll(p_rw, 1, axis=0))
    pr_ref[0] = p_rw + (p_prev - p_rw) * mix_ref[...]
    last_ref[0:1, :] = p_rw[tm - 1:tm, :]
    p = proj[:, :n_mla]
    nh, nope = MLA_HEADS, MLA_NOPE
    ang = pos_ref[0] * cs_ref[0:1, :]
    lane = lax.broadcasted_iota(jnp.int32, ang.shape, 1)
    low = lane < MLA_ROPE
    cs = jnp.where(low, jnp.cos(ang), cs_ref[1:2, :] * jnp.sin(ang))

    def rope(xpair):
        prod = xpair * cs
        return jnp.where(low, prod + pltpu.roll(prod, MLA_ROPE, axis=1), 0.0)

    q = _dot(_rms(p[:, :Q_LORA], qg_ref[...]), wq_ref[...]) * scale
    kv = _dot(_rms(p[:, Q_LORA:Q_LORA + KV_LORA], kvg_ref[...]), wkv_ref[...])
    k_rope = rope(p[:, Q_LORA + KV_LORA:]).astype(BF16)
    for h in range(nh):
        o = 2 * LANES * h
        q_ref[0, :, o:o + nope] = q[:, nope * h:nope * (h + 1)].astype(BF16)
        q_ref[0, :, o + nope:o + 2 * LANES] = rope(
            q[:, nh * nope + LANES * h:nh * nope + LANES * (h + 1)]).astype(BF16)
        k_ref[0, :, o:o + nope] = kv[:, nope * h:nope * (h + 1)].astype(BF16)
        k_ref[0, :, o + nope:o + 2 * LANES] = k_rope
        v_ref[0, :, o:o + MLA_V] = kv[:, nh * nope + MLA_V * h:nh * nope + MLA_V * (h + 1)].astype(BF16)
        v_ref[0, :, o + MLA_V:o + 2 * LANES] = (lane == 0).astype(BF16)


def _inproj(h, mod, ng, w, n_mla, pos, cs, qg, kvg, wq, wkv, mix):
    bsz, s, d = h.shape
    n = w.shape[1]
    tm = min(FFN_ROWS, s)
    row_spec = lambda wd_: pl.BlockSpec((1, tm, wd_), lambda b, i: (b, i, 0))
    qk_w = MLA_HEADS * 2 * LANES
    v_w = qk_w
    return pl.pallas_call(
        functools.partial(_inproj_kernel, n_mla=n_mla, scale=(MLA_NOPE + MLA_ROPE) ** -0.5 * LOG2_E),
        grid=(bsz, s // tm),
        in_specs=[row_spec(d), pl.BlockSpec((1, N_MOD, d), lambda b, i: (b, 0, 0)),
                  _const_spec((1, d)), _const_spec((d, n)), row_spec(1), _const_spec((2, LANES)),
                  _const_spec((1, Q_LORA)), _const_spec((1, KV_LORA)),
                  _const_spec(wq.shape), _const_spec(wkv.shape), _const_spec((1, n - n_mla))],
        out_specs=[row_spec(n - n_mla), row_spec(qk_w), row_spec(qk_w), row_spec(v_w)],
        out_shape=[jax.ShapeDtypeStruct((bsz, s, n - n_mla), F32),
                   jax.ShapeDtypeStruct((bsz, s, qk_w), BF16),
                   jax.ShapeDtypeStruct((bsz, s, qk_w), BF16),
                   jax.ShapeDtypeStruct((bsz, s, v_w), BF16)],
        scratch_shapes=[pltpu.VMEM((8, n - n_mla), F32)],
        compiler_params=_params(("parallel", "arbitrary")),
        name="inproj",
    )(h, mod, ng.reshape(1, d), w, pos, cs, qg.reshape(1, -1), kvg.reshape(1, -1), wq, wkv,
      mix.reshape(1, -1))


def _flash_kernel(q_ref, k_ref, v_ref, o_ref, *, bq):
    i = pl.program_id(2)
    bk = 2 * bq
    base = pl.multiple_of(i * bk, bk)

    def scores(chain, start, n):
        kb = k_ref[0, pl.ds(start, n), :]
        return lax.dot_general(q_ref[0, chain * bq:(chain + 1) * bq, :], kb, (((1,), (1,)), ((), ())),
                               preferred_element_type=F32)

    def first(s, start, n):
        m = jnp.max(s, axis=-1, keepdims=True)
        p = jnp.exp2((s - m).astype(BF16))
        return m, jnp.dot(p, v_ref[0, pl.ds(start, n), :], preferred_element_type=F32)

    def update(carry, s, start, n):
        m, acc = carry
        m_new = jnp.maximum(m, jnp.max(s, axis=-1, keepdims=True))
        alpha = jnp.exp2(m - m_new)
        p = jnp.exp2((s - m_new).astype(BF16))
        return m_new, alpha * acc + jnp.dot(p, v_ref[0, pl.ds(start, n), :], preferred_element_type=F32)

    visible = (lax.broadcasted_iota(jnp.int32, (bq, bq), 1) // ATTN_CHUNK
               <= lax.broadcasted_iota(jnp.int32, (bq, bq), 0) // ATTN_CHUNK)
    mid = pl.multiple_of(base + bq, bq)
    s_a = jnp.where(visible, scores(0, base, bq), -1e30)
    s_b = jnp.where(visible, scores(1, mid, bq), -1e30)
    s_b0 = scores(1, base, bq)
    chain_a = first(s_a, base, bq)
    chain_b = update(first(s_b, mid, bq), s_b0, base, bq)

    def block(j, carry):
        chain_a, chain_b = carry
        start = pl.multiple_of(j * bk, bk)
        s_a = scores(0, start, bk)
        s_b = scores(1, start, bk)
        return update(chain_a, s_a, start, bk), update(chain_b, s_b, start, bk)

    pair = lambda jj, carry: block(2 * jj + 1, block(2 * jj, carry))
    carry = lax.fori_loop(0, i // 2, pair, (chain_a, chain_b))
    chain_a, chain_b = lax.fori_loop(0, i % 2, lambda _, carry: block(i - 1, carry), carry)
    for chain, (_, acc) in enumerate((chain_a, chain_b)):
        o_ref[0, chain * bq:(chain + 1) * bq, :] = acc[:, :MLA_V] / acc[:, MLA_V:MLA_V + 1]


def _flash(q, k, v):
    bsz, s, _ = q.shape
    bq = min(ATTN_CHAIN_ROWS, s // 2)
    qk_w = 2 * LANES
    return pl.pallas_call(
        functools.partial(_flash_kernel, bq=bq),
        grid=(bsz, MLA_HEADS, s // (2 * bq)),
        in_specs=[pl.BlockSpec((1, 2 * bq, qk_w), lambda b, h, i: (b, i, h)),
                  pl.BlockSpec((1, s, qk_w), lambda b, h, i: (b, 0, h)),
                  pl.BlockSpec((1, s, qk_w), lambda b, h, i: (b, 0, h))],
        out_specs=pl.BlockSpec((1, 2 * bq, MLA_V), lambda b, h, i: (b, i, h)),
        out_shape=jax.ShapeDtypeStruct((bsz, s, MLA_HEADS * MLA_V), F32),
        compiler_params=_params(("parallel", "parallel", "arbitrary")),
        name="flash",
    )(q, k, v)


def _pair_diag(y, pairmask):
    yb = y.astype(BF16)
    return jnp.where(pairmask, jnp.concatenate([yb, yb], axis=1), jnp.zeros((), BF16))


def _unit_lower_inverse(a, eye, blockdiag, pairmask):
    pw = a.shape[-1]
    diag = lambda y: _pair_diag(y, pairmask)
    mm = lambda x, y: _bdot(x, diag(y))

    def mm2(x, y1, y2):
        both = _bdot(x, jnp.concatenate([diag(y1), diag(y2)], axis=2))
        return both[:, :, :pw], both[:, :, pw:]

    ad = jnp.where(blockdiag, a, 0.0)
    ao = a - ad
    a2 = mm(ad, ad)
    a4, a23 = mm2(a2, a2, eye + ad)
    t1 = eye + ad + a23
    a8, a4t1 = mm2(a4, a4, t1)
    t2 = t1 + a4t1
    td = t2 + mm(a8, t2)
    n1 = mm(td, ao)
    n1td, n2 = mm2(n1, td, n1)
    w = td + n1td
    return w + mm(n2, w)


def _rwkv_kernel(p_ref, w0_ref, a0_ref, kk_ref, ka_ref, rk_ref, lnw_ref, lnb_ref,
                 wl_ref, g2_ref, ones_ref, tri_ref, o_ref, state_ref, y_ref, *, nseq, nchunks):
    t = pl.program_id(0)
    nl = RWKV_CHUNK
    hd = RWKV_HEAD
    c_w = o_ref.shape[-1]
    nheads = c_w // hd

    @pl.when(t == 0)
    def _():
        state_ref[...] = jnp.zeros_like(state_ref)

    p = p_ref[...].reshape(nseq * nchunks * nl, p_ref.shape[-1])

    r = p[:, :c_w]
    k = p[:, c_w:2 * c_w]
    v = p[:, 2 * c_w:3 * c_w]
    xwa = p[:, 3 * c_w:3 * c_w + LANES]
    xg = p[:, 3 * c_w + LANES:]

    def segsum(x, terms):
        halves = []
        for o in range(0, c_w, 2 * LANES):
            xo = x[:, o:o + 2 * LANES]
            pieces = _split2(xo) if terms == 2 else (xo.astype(BF16),)
            halves.append(sum(jnp.dot(pc, ones_ref[...], preferred_element_type=F32) for pc in pieces))
        return jnp.concatenate(halves, axis=1)

    def rsqrt(x):
        return jnp.exp2(-0.5 * jnp.log2(x))

    def sigmoid(x):
        return 0.5 * jnp.tanh(0.5 * x) + 0.5

    lane = lax.broadcasted_iota(jnp.int32, xwa.shape, 1)
    wa = _dot(jnp.where(lane < DECAY_LORA, jnp.tanh(xwa), xwa), wl_ref[...])
    z = w0_ref[...] + wa[:, :c_w]
    lw = -EXP_NEG_HALF * sigmoid(z)
    a = sigmoid(a0_ref[...] + wa[:, c_w:])
    g = _dot(sigmoid(xg), g2_ref[...])
    kk = k * kk_ref[...]
    kk = kk * rsqrt(jnp.maximum(segsum(kk * kk, 1), 1e-24))
    k2 = k * (1.0 + (a - 1.0) * ka_ref[...])
    bb = kk * a

    tri = tri_ref[...]
    grp = tri.shape[0]
    parts = _split3(lw)
    c = jnp.concatenate(
        [sum(jnp.dot(tri, part[o:o + grp], preferred_element_type=F32) for part in parts)
         for o in range(0, lw.shape[0], grp)], axis=0)
    slots = [(b, j) for b in range(nseq) for j in range(nchunks)]
    rows = lambda x, b, j: x[nl * (b * nchunks + j):nl * (b * nchunks + j + 1)]
    e_last = {bj: jnp.exp(rows(c, *bj)[nl - 1:nl, :]) for bj in slots}
    e_neg = jnp.exp(-c)
    e_rem = jnp.concatenate([e_last[bj] * rows(e_neg, *bj) for bj in slots], axis=0)
    a_t = -kk * jnp.exp(c - lw)
    r_t = r * jnp.exp(c)
    b_t = bb * e_neg
    k_t = k2 * e_neg
    b_p = bb * e_rem
    k_p = k2 * e_rem

    pw = 2 * hd
    npairs = nheads // 2

    def pairs(x):
        return jnp.stack([x[:, pw * q:pw * (q + 1)] for q in range(npairs)], axis=0)

    def all_pairs(x):
        return jnp.concatenate([pairs(rows(x, b, j)) for j in range(nchunks) for b in range(nseq)], axis=0)

    ri = lax.broadcasted_iota(jnp.int32, (1, nl, pw), 1)
    ci = lax.broadcasted_iota(jnp.int32, (1, nl, pw), 2) % hd
    strict = ci < ri
    incl = ci <= ri
    eye = (ci == ri).astype(F32)
    blockdiag = (ri // INV_BLOCK) == (ci // INV_BLOCK)
    pairmask = (lax.broadcasted_iota(jnp.int32, (1, pw, pw), 1) // hd
                == lax.broadcasted_iota(jnp.int32, (1, pw, pw), 2) // hd)

    a_p = all_pairs(a_t)
    r_p = all_pairs(r_t)
    ar = jnp.concatenate([a_p, r_p], axis=1)
    g_bk = _bdot_nt(ar, jnp.concatenate([_pair_diag(all_pairs(b_t), pairmask),
                                         _pair_diag(all_pairs(k_t), pairmask)], axis=1))
    gb, gk = g_bk[:, :, :pw], g_bk[:, :, pw:]
    a_rb = jnp.where(incl, gb[:, nl:], 0.0)
    a_ak = jnp.where(strict, gk[:, :nl], 0.0)
    a_rk = jnp.where(incl, gk[:, nl:], 0.0)
    tinv = _unit_lower_inverse(jnp.where(strict, gb[:, :nl], 0.0), eye, blockdiag, pairmask)
    v_p = all_pairs(v)
    v_d = _pair_diag(v_p, pairmask)
    akv = _bdot(a_ak, v_d)
    rkv = _bdot(a_rk, v_d)
    kpv = _bdot_tn(v_p, all_pairs(k_p))
    b_pp = all_pairs(b_p).astype(BF16)
    ta_u0 = _bdot(tinv, jnp.concatenate([_pair_diag(a_p, pairmask), _pair_diag(akv, pairmask)], axis=2))
    u0 = ta_u0[:, :, pw:]
    tar = jnp.concatenate([ta_u0[:, :, :pw], r_p], axis=1).astype(BF16)

    st = state_ref[...]
    per_chunk = nseq * npairs
    for j in range(nchunks):
        sj = slice(per_chunk * j, per_chunk * (j + 1))
        sz = _bdot_nt(tar[sj], st)
        u = sz[:, :nl] + u0[sj]
        y_j = sz[:, nl:] + _bdot(a_rb[sj], _pair_diag(u, pairmask)) + rkv[sj]
        for b in range(nseq):
            r0 = nl * (b * nchunks + j)
            for q in range(npairs):
                y_ref[r0:r0 + nl, pw * q:pw * (q + 1)] = y_j[b * npairs + q]
        decay = jnp.concatenate([pairs(e_last[(b, j)]) for b in range(nseq)], axis=0)
        st = jnp.where(pairmask, st * decay + _bdot_tn(u, b_pp[sj]) + kpv[sj], 0.0)
    state_ref[...] = st

    y = y_ref[...]
    inv_n = 1.0 / hd
    mu = segsum(y, 2) * inv_n
    d = y - mu
    var = segsum(d * d, 1) * inv_n
    yn = d * rsqrt(var + GN_EPS) * lnw_ref[...] + lnb_ref[...]
    bonus = segsum(r * k2 * rk_ref[...], 1) * v
    o_ref[...] = ((yn + bonus) * g).reshape(o_ref.shape)


def _rwkv(p_rw, w0, a0, k_k, k_a, r_k, ln_w, ln_b, wl, g2):
    bsz, s, n_rw = p_rw.shape
    c_w = w0.shape[-1]
    nchunks = min(RWKV_CHUNKS_PER_STEP, s // RWKV_CHUNK)
    nl = RWKV_CHUNK * nchunks
    nheads = c_w // RWKV_HEAD
    seg = jnp.arange(2 * LANES) // RWKV_HEAD
    ones_bd = (seg[:, None] == seg[None, :]).astype(BF16)
    pos = jnp.arange(math.gcd(bsz * nl, RWKV_CUMSUM_ROWS))
    tri = ((pos[None, :] <= pos[:, None])
           & (pos[None, :] // RWKV_CHUNK == pos[:, None] // RWKV_CHUNK)).astype(BF16)
    vec = lambda x: x.reshape(1, -1)
    return pl.pallas_call(
        functools.partial(_rwkv_kernel, nseq=bsz, nchunks=nchunks),
        grid=(s // nl,),
        in_specs=[pl.BlockSpec((bsz, nl, n_rw), lambda t: (0, t, 0))] + [_const_spec((1, c_w))] * 7
                 + [_const_spec(wl.shape), _const_spec(g2.shape), _const_spec(ones_bd.shape),
                    _const_spec(tri.shape)],
        out_specs=pl.BlockSpec((bsz, nl, c_w), lambda t: (0, t, 0)),
        out_shape=jax.ShapeDtypeStruct((bsz, s, c_w), F32),
        scratch_shapes=[pltpu.VMEM((bsz * (nheads // 2), 2 * RWKV_HEAD, 2 * RWKV_HEAD), F32),
                        pltpu.VMEM((bsz * nl, c_w), F32)],
        compiler_params=_params(("arbitrary",)),
        name="rwkv",
    )(p_rw, vec(w0), vec(a0), vec(k_k), vec(k_a), vec(r_k), vec(ln_w), vec(ln_b),
      wl, g2, ones_bd, tri)


def _swap_halves(w):
    half = w.shape[-1] // 2
    return jnp.concatenate([w[..., half:], w[..., :half]], axis=-1)


def kernel(x, c, positions, w_mod, b_mod, ffn1_norm_g, ffn1_w_gate, ffn1_w_up, ffn1_w_down, mix_norm_g, w_in, q_norm_g, w_uq, kv_norm_g, w_ukv, attn_out_norm_g, rwkv_shift_mix, rwkv_w0, rwkv_w2, rwkv_a0, rwkv_a2, rwkv_g2, rwkv_k_k, rwkv_k_a, rwkv_r_k, rwkv_ln_w, rwkv_ln_b, w_out, ffn2_norm_g, ffn2_w_gate, ffn2_w_up, ffn2_w_down, final_norm_g):
    depth = w_mod.shape[0]
    bsz, s, d = x.shape
    nh, nope, rope_w, vw = MLA_HEADS, MLA_NOPE, MLA_ROPE, MLA_V
    mla_cols = Q_LORA + KV_LORA + rope_w
    mla_w = MLA_HEADS * MLA_V

    half = rope_w // 2
    inv_freq = ROPE_BASE ** (-jnp.arange(half, dtype=F32) / half)
    sine_sign = jnp.where(jnp.arange(LANES) < 3 * half, -1.0, 1.0).astype(F32)
    cs = jnp.stack([jnp.tile(inv_freq, LANES // half), sine_sign])
    pos = positions.astype(F32)[..., None]

    h = x
    for l in range(depth):
        w_kr = w_in[l][:, Q_LORA + KV_LORA:mla_cols]
        w_in_l = jnp.concatenate([w_in[l][:, :mla_cols], _swap_halves(w_kr), w_in[l][:, mla_cols:]],
                                 axis=1).astype(BF16)
        n_mla = mla_cols + rope_w
        wq = w_uq[l].reshape(Q_LORA, nh, nope + rope_w)
        wq_rope = wq[:, :, nope:]
        wq_l = jnp.concatenate(
            [wq[:, :, :nope].reshape(Q_LORA, nh * nope),
             jnp.concatenate([wq_rope, _swap_halves(wq_rope)], axis=-1).reshape(Q_LORA, nh * 2 * rope_w)],
            axis=1).astype(BF16)
        wkv = w_ukv[l].reshape(KV_LORA, nh, nope + vw)
        wkv_l = jnp.concatenate([wkv[:, :, :nope].reshape(KV_LORA, nh * nope),
                                 wkv[:, :, nope:].reshape(KV_LORA, nh * vw)], axis=1).astype(BF16)
        c_w = rwkv_w0.shape[-1]
        zeros = jnp.zeros((DECAY_LORA, c_w), F32)
        wl = jnp.concatenate([jnp.concatenate([rwkv_w2[l], zeros], axis=1),
                              jnp.concatenate([zeros, rwkv_a2[l]], axis=1)], axis=0).astype(BF16)

        mod = _mod(c, w_mod[l], b_mod[l]).reshape(bsz, N_MOD, d)
        h = _ffn(h, mod, ffn1_norm_g[l], ffn1_w_gate[l].astype(BF16), ffn1_w_up[l].astype(BF16),
                 ffn1_w_down[l].astype(BF16), mod_rows=(0, 1, 2))
        p_rw, q, k, v = _inproj(h, mod, mix_norm_g[l], w_in_l, n_mla, pos, cs, q_norm_g[l], kv_norm_g[l],
                                wq_l, wkv_l, rwkv_shift_mix[l])
        y_a = _flash(q, k, v)
        y_b = _rwkv(p_rw, rwkv_w0[l], rwkv_a0[l], rwkv_k_k[l], rwkv_k_a[l],
                    rwkv_r_k[l].reshape(-1), rwkv_ln_w[l], rwkv_ln_b[l], wl, rwkv_g2[l].astype(BF16))
        mix = (y_a, y_b, attn_out_norm_g[l], w_out[l][:mla_w].astype(BF16), w_out[l][mla_w:].astype(BF16))
        h = _ffn(h, mod, ffn2_norm_g[l], ffn2_w_gate[l].astype(BF16), ffn2_w_up[l].astype(BF16),
                 ffn2_w_down[l].astype(BF16), mod_rows=(6, 7, 8, 5), mix=mix,
                 final_g=final_norm_g if l == depth - 1 else None)
    return h
```

```python
import functools
import math

import jax
import jax.numpy as jnp
from jax import lax
from jax.experimental import pallas as pl
from jax.experimental.pallas import tpu as pltpu

F32 = jnp.float32
BF16 = jnp.bfloat16

MLA_HEADS = 4
MLA_NOPE = 128
MLA_ROPE = 64
MLA_V = 128
Q_LORA = 384
KV_LORA = 256
ROPE_BASE = 10000.0
RWKV_HEAD = 64
DECAY_LORA = 64
AAA_LORA = 64
GATE_LORA = 128
NORM_EPS = 1e-6
GN_EPS = 64e-5
N_MOD = 9
ATTN_CHUNK = 64
LOG2_E = 1.4426950408889634
EXP_NEG_HALF = 0.6065306597126334

LANES = 128
VMEM_LIMIT_BYTES = 56 * 1024 * 1024

FFN_ROWS = 512
ATTN_CHAIN_ROWS = 1024
RWKV_CHUNK = 64
RWKV_CHUNKS_PER_STEP = 2
RWKV_CUMSUM_ROWS = 256
INV_BLOCK = 16


def _dot(a, b):
    return jnp.dot(a.astype(BF16), b.astype(BF16), preferred_element_type=F32)


def _batched(a, b, ca, cb):
    return lax.dot_general(a.astype(BF16), b.astype(BF16), (((ca,), (cb,)), ((0,), (0,))),
                           preferred_element_type=F32)


def _bdot(a, b):
    return _batched(a, b, 2, 1)


def _bdot_nt(a, b):
    return _batched(a, b, 2, 2)


def _bdot_tn(a, b):
    return _batched(a, b, 1, 1)


def _split2(x):
    hi = x.astype(BF16)
    lo = (x - hi.astype(F32)).astype(BF16)
    return hi, lo


def _split3(x):
    hi = x.astype(BF16)
    r1 = x - hi.astype(F32)
    mid = r1.astype(BF16)
    lo = (r1 - mid.astype(F32)).astype(BF16)
    return hi, mid, lo


def _sigmoid(x):
    return 1.0 / (1.0 + jnp.exp(-x))


def _rms(x, g):
    return x * lax.rsqrt(jnp.mean(x * x, axis=-1, keepdims=True) + NORM_EPS) * g


def _const_spec(shape):
    nd = len(shape)
    return pl.BlockSpec(shape, lambda *_: (0,) * nd, pipeline_mode=pl.Buffered(1))


def _params(semantics):
    return pltpu.CompilerParams(dimension_semantics=semantics, vmem_limit_bytes=VMEM_LIMIT_BYTES)


def _mod_kernel(c_ref, w_ref, b_ref, o_ref):
    c = c_ref[...]
    c_act = c * _sigmoid(c)
    o_ref[...] = _dot(c_act, w_ref[...]) + b_ref[...]


def _mod(c, w_mod, b_mod):
    bsz, d = c.shape
    n = w_mod.shape[1]
    tn = d
    return pl.pallas_call(
        _mod_kernel,
        grid=(n // tn,),
        in_specs=[pl.BlockSpec((bsz, d), lambda j: (0, 0)),
                  pl.BlockSpec((d, tn), lambda j: (0, j)),
                  pl.BlockSpec((1, tn), lambda j: (0, j))],
        out_specs=pl.BlockSpec((bsz, tn), lambda j: (0, j)),
        out_shape=jax.ShapeDtypeStruct((bsz, n), F32),
        compiler_params=_params(("arbitrary",)),
        name="mod",
    )(c, w_mod, b_mod.reshape(1, n))


def _ffn_kernel(*refs, mod_rows, pre_mix, final_norm):
    it = iter(refs)
    h_ref, mod_ref, ng_ref, wg_ref, wu_ref, wd_ref = (next(it) for _ in range(6))
    if pre_mix:
        ya_ref, yb_ref, ang_ref, woa_ref, wob_ref = (next(it) for _ in range(5))
    if final_norm:
        fng_ref = next(it)
    o_ref = next(it)

    h = h_ref[0]
    mod = mod_ref[0]
    row = lambda i: mod[i:i + 1, :]
    if pre_mix:
        ya = _rms(ya_ref[0], ang_ref[...])
        y = _dot(ya, woa_ref[...]) + _dot(yb_ref[0], wob_ref[...])
        h = h + row(mod_rows[3]) * y
    sh, sc, gt = row(mod_rows[0]), row(mod_rows[1]), row(mod_rows[2])
    u = (_rms(h, ng_ref[...]) * (1.0 + sc) + sh).astype(BF16)
    g = jnp.dot(u, wg_ref[...], preferred_element_type=F32)
    up = jnp.dot(u, wu_ref[...], preferred_element_type=F32)
    act = (g * _sigmoid(g) * up).astype(BF16)
    f = jnp.dot(act, wd_ref[...], preferred_element_type=F32)
    h = h + 0.5 * gt * f
    if final_norm:
        h = _rms(h, fng_ref[...])
    o_ref[0] = h


def _ffn(h, mod, ng, wg, wu, wd, *, mod_rows, mix=None, final_g=None):
    bsz, s, d = h.shape
    f = wg.shape[1]
    tm = min(FFN_ROWS, s)
    row_spec = lambda w: pl.BlockSpec((1, tm, w), lambda b, i: (b, i, 0))
    in_specs = [row_spec(d),
                pl.BlockSpec((1, N_MOD, d), lambda b, i: (b, 0, 0)),
                _const_spec((1, d)), _const_spec((d, f)), _const_spec((d, f)), _const_spec((f, d))]
    args = [h, mod, ng.reshape(1, d), wg, wu, wd]
    if mix is not None:
        ya, yb, ang, woa, wob = mix
        wa, wb = ya.shape[-1], yb.shape[-1]
        in_specs += [row_spec(wa), row_spec(wb), _const_spec((1, wa)),
                     _const_spec((wa, d)), _const_spec((wb, d))]
        args += [ya, yb, ang.reshape(1, wa), woa, wob]
    if final_g is not None:
        in_specs.append(_const_spec((1, d)))
        args.append(final_g.reshape(1, d))
    kern = functools.partial(_ffn_kernel, mod_rows=mod_rows, pre_mix=mix is not None,
                             final_norm=final_g is not None)
    return pl.pallas_call(
        kern,
        grid=(bsz, s // tm),
        in_specs=in_specs,
        out_specs=row_spec(d),
        out_shape=jax.ShapeDtypeStruct((bsz, s, d), F32),
        compiler_params=_params(("parallel", "parallel")),
        name="ffn_mix" if mix is not None else "ffn",
    )(*args)


def _inproj_kernel(h_ref, mod_ref, ng_ref, w_ref, pos_ref, cs_ref, qg_ref, kvg_ref, wq_ref, wkv_ref,
                   mix_ref, pr_ref, q_ref, k_ref, v_ref, last_ref, *, n_mla, scale):
    @pl.when(pl.program_id(1) == 0)
    def _():
        last_ref[...] = jnp.zeros_like(last_ref)

    mod = mod_ref[0]
    u = _rms(h_ref[0], ng_ref[...]) * (1.0 + mod[4:5, :]) + mod[3:4, :]
    proj = _dot(u, w_ref[...])
    p_rw = proj[:, n_mla:]
    tm = p_rw.shape[0]
    rowi = lax.broadcasted_iota(jnp.int32, p_rw.shape, 0)
    p_prev = jnp.where(rowi == 0, last_ref[0:1, :], pltpu.roll(p_rw, 1, axis=0))
    pr_ref[0] = p_rw + (p_prev - p_rw) * mix_ref[...]
    last_ref[0:1, :] = p_rw[tm - 1:tm, :]
    p = proj[:, :n_mla]
    nh, nope = MLA_HEADS, MLA_NOPE
    ang = pos_ref[0] * cs_ref[0:1, :]
    lane = lax.broadcasted_iota(jnp.int32, ang.shape, 1)
    low = lane < MLA_ROPE
    cs = jnp.where(low, jnp.cos(ang), cs_ref[1:2, :] * jnp.sin(ang))

    def rope(xpair):
        prod = xpair * cs
        return jnp.where(low, prod + pltpu.roll(prod, MLA_ROPE, axis=1), 0.0)

    q = _dot(_rms(p[:, :Q_LORA], qg_ref[...]), wq_ref[...]) * scale
    kv = _dot(_rms(p[:, Q_LORA:Q_LORA + KV_LORA], kvg_ref[...]), wkv_ref[...])
    k_rope = rope(p[:, Q_LORA + KV_LORA:]).astype(BF16)
    for h in range(nh):
        o = 2 * LANES * h
        q_ref[0, :, o:o + nope] = q[:, nope * h:nope * (h + 1)].astype(BF16)
        q_ref[0, :, o + nope:o + 2 * LANES] = rope(
            q[:, nh * nope + LANES * h:nh * nope + LANES * (h + 1)]).astype(BF16)
        k_ref[0, :, o:o + nope] = kv[:, nope * h:nope * (h + 1)].astype(BF16)
        k_ref[0, :, o + nope:o + 2 * LANES] = k_rope
        v_ref[0, :, o:o + MLA_V] = kv[:, nh * nope + MLA_V * h:nh * nope + MLA_V * (h + 1)].astype(BF16)
        v_ref[0, :, o + MLA_V:o + 2 * LANES] = (lane == 0).astype(BF16)


def _inproj(h, mod, ng, w, n_mla, pos, cs, qg, kvg, wq, wkv, mix):
    bsz, s, d = h.shape
    n = w.shape[1]
    tm = min(FFN_ROWS, s)
    row_spec = lambda wd_: pl.BlockSpec((1, tm, wd_), lambda b, i: (b, i, 0))
    qk_w = MLA_HEADS * 2 * LANES
    v_w = qk_w
    return pl.pallas_call(
        functools.partial(_inproj_kernel, n_mla=n_mla, scale=(MLA_NOPE + MLA_ROPE) ** -0.5 * LOG2_E),
        grid=(bsz, s // tm),
        in_specs=[row_spec(d), pl.BlockSpec((1, N_MOD, d), lambda b, i: (b, 0, 0)),
                  _const_spec((1, d)), _const_spec((d, n)), row_spec(1), _const_spec((2, LANES)),
                  _const_spec((1, Q_LORA)), _const_spec((1, KV_LORA)),
                  _const_spec(wq.shape), _const_spec(wkv.shape), _const_spec((1, n - n_mla))],
        out_specs=[row_spec(n - n_mla), row_spec(qk_w), row_spec(qk_w), row_spec(v_w)],
        out_shape=[jax.ShapeDtypeStruct((bsz, s, n - n_mla), F32),
                   jax.ShapeDtypeStruct((bsz, s, qk_w), BF16),
                   jax.ShapeDtypeStruct((bsz, s, qk_w), BF16),
                   jax.ShapeDtypeStruct((bsz, s, v_w), BF16)],
        scratch_shapes=[pltpu.VMEM((8, n - n_mla), F32)],
        compiler_params=_params(("parallel", "arbitrary")),
        name="inproj",
    )(h, mod, ng.reshape(1, d), w, pos, cs, qg.reshape(1, -1), kvg.reshape(1, -1), wq, wkv,
      mix.reshape(1, -1))


def _flash_kernel(q_ref, k_ref, v_ref, o_ref, *, bq, nsteps):
    bk = 2 * bq
    for step in range(nsteps):
        pl.when(pl.program_id(2) == step)(
            functools.partial(_flash_step, q_ref, k_ref, v_ref, o_ref, bq=bq, step=step))


def _flash_step(q_ref, k_ref, v_ref, o_ref, *, bq, step):
    bk = 2 * bq
    base = step * bk

    def scores(chain, start, n):
        kb = k_ref[0, pl.ds(start, n), :]
        return lax.dot_general(q_ref[0, chain * bq:(chain + 1) * bq, :], kb, (((1,), (1,)), ((), ())),
                               preferred_element_type=F32)

    def first(s, start, n):
        m = jnp.max(s, axis=-1, keepdims=True)
        p = jnp.exp2((s - m).astype(BF16))
        return m, jnp.dot(p, v_ref[0, pl.ds(start, n), :], preferred_element_type=F32)

    def update(carry, s, start, n):
        m, acc = carry
        m_new = jnp.maximum(m, jnp.max(s, axis=-1, keepdims=True))
        alpha = jnp.exp2(m - m_new)
        p = jnp.exp2((s - m_new).astype(BF16))
        return m_new, alpha * acc + jnp.dot(p, v_ref[0, pl.ds(start, n), :], preferred_element_type=F32)

    visible = (lax.broadcasted_iota(jnp.int32, (bq, bq), 1) // ATTN_CHUNK
               <= lax.broadcasted_iota(jnp.int32, (bq, bq), 0) // ATTN_CHUNK)
    mid = base + bq
    s_a = jnp.where(visible, scores(0, base, bq), -1e30)
    s_b = jnp.where(visible, scores(1, mid, bq), -1e30)
    s_b0 = scores(1, base, bq)
    chain_a = first(s_a, base, bq)
    chain_b = update(first(s_b, mid, bq), s_b0, base, bq)

    for j in range(step):
        start = j * bk
        s_a = scores(0, start, bk)
        s_b = scores(1, start, bk)
        chain_a, chain_b = update(chain_a, s_a, start, bk), update(chain_b, s_b, start, bk)
    for chain, (_, acc) in enumerate((chain_a, chain_b)):
        o_ref[0, chain * bq:(chain + 1) * bq, :] = acc[:, :MLA_V] / acc[:, MLA_V:MLA_V + 1]


def _flash(q, k, v):
    bsz, s, _ = q.shape
    bq = min(ATTN_CHAIN_ROWS, s // 2)
    qk_w = 2 * LANES
    return pl.pallas_call(
        functools.partial(_flash_kernel, bq=bq, nsteps=s // (2 * bq)),
        grid=(bsz, MLA_HEADS, s // (2 * bq)),
        in_specs=[pl.BlockSpec((1, 2 * bq, qk_w), lambda b, h, i: (b, i, h)),
                  pl.BlockSpec((1, s, qk_w), lambda b, h, i: (b, 0, h)),
                  pl.BlockSpec((1, s, qk_w), lambda b, h, i: (b, 0, h))],
        out_specs=pl.BlockSpec((1, 2 * bq, MLA_V), lambda b, h, i: (b, i, h)),
        out_shape=jax.ShapeDtypeStruct((bsz, s, MLA_HEADS * MLA_V), F32),
        compiler_params=_params(("parallel", "parallel", "arbitrary")),
        name="flash",
    )(q, k, v)


def _pair_diag(y, pairmask):
    yb = y.astype(BF16)
    return jnp.where(pairmask, jnp.concatenate([yb, yb], axis=1), jnp.zeros((), BF16))


def _unit_lower_inverse(a, eye, blockdiag, pairmask):
    pw = a.shape[-1]
    diag = lambda y: _pair_diag(y, pairmask)
    mm = lambda x, y: _bdot(x, diag(y))

    def mm2(x, y1, y2):
        both = _bdot(x, jnp.concatenate([diag(y1), diag(y2)], axis=2))
        return both[:, :, :pw], both[:, :, pw:]

    ad = jnp.where(blockdiag, a, 0.0)
    ao = a - ad
    a2 = mm(ad, ad)
    a4, a23 = mm2(a2, a2, eye + ad)
    t1 = eye + ad + a23
    a8, a4t1 = mm2(a4, a4, t1)
    t2 = t1 + a4t1
    td = t2 + mm(a8, t2)
    n1 = mm(td, ao)
    n1td, n2 = mm2(n1, td, n1)
    w = td + n1td
    return w + mm(n2, w)


def _rwkv_kernel(p_ref, w0_ref, a0_ref, kk_ref, ka_ref, rk_ref, lnw_ref, lnb_ref,
                 wl_ref, g2_ref, ones_ref, tri_ref, o_ref, state_ref, y_ref, *, nseq, nchunks):
    t = pl.program_id(0)
    nl = RWKV_CHUNK
    hd = RWKV_HEAD
    c_w = o_ref.shape[-1]
    nheads = c_w // hd

    @pl.when(t == 0)
    def _():
        state_ref[...] = jnp.zeros_like(state_ref)

    p = p_ref[...].reshape(nseq * nchunks * nl, p_ref.shape[-1])

    r = p[:, :c_w]
    k = p[:, c_w:2 * c_w]
    v = p[:, 2 * c_w:3 * c_w]
    xwa = p[:, 3 * c_w:3 * c_w + LANES]
    xg = p[:, 3 * c_w + LANES:]

    def segsum(x, terms):
        halves = []
        for o in range(0, c_w, 2 * LANES):
            xo = x[:, o:o + 2 * LANES]
            pieces = _split2(xo) if terms == 2 else (xo.astype(BF16),)
            halves.append(sum(jnp.dot(pc, ones_ref[...], preferred_element_type=F32) for pc in pieces))
        return jnp.concatenate(halves, axis=1)

    def rsqrt(x):
        return jnp.exp2(-0.5 * jnp.log2(x))

    def sigmoid(x):
        return 0.5 * jnp.tanh(0.5 * x) + 0.5

    lane = lax.broadcasted_iota(jnp.int32, xwa.shape, 1)
    wa = _dot(jnp.where(lane < DECAY_LORA, jnp.tanh(xwa), xwa), wl_ref[...])
    z = w0_ref[...] + wa[:, :c_w]
    lw = -EXP_NEG_HALF * sigmoid(z)
    a = sigmoid(a0_ref[...] + wa[:, c_w:])
    g = _dot(sigmoid(xg), g2_ref[...])
    kk = k * kk_ref[...]
    kk = kk * rsqrt(jnp.maximum(segsum(kk * kk, 1), 1e-24))
    k2 = k * (1.0 + (a - 1.0) * ka_ref[...])
    bb = kk * a

    tri = tri_ref[...]
    grp = tri.shape[0]
    parts = _split3(lw)
    c = jnp.concatenate(
        [sum(jnp.dot(tri, part[o:o + grp], preferred_element_type=F32) for part in parts)
         for o in range(0, lw.shape[0], grp)], axis=0)
    slots = [(b, j) for b in range(nseq) for j in range(nchunks)]
    rows = lambda x, b, j: x[nl * (b * nchunks + j):nl * (b * nchunks + j + 1)]
    e_last = {bj: jnp.exp(rows(c, *bj)[nl - 1:nl, :]) for bj in slots}
    e_neg = jnp.exp(-c)
    e_rem = jnp.concatenate([e_last[bj] * rows(e_neg, *bj) for bj in slots], axis=0)
    a_t = -kk * jnp.exp(c - lw)
    r_t = r * jnp.exp(c)
    b_t = bb * e_neg
    k_t = k2 * e_neg
    b_p = bb * e_rem
    k_p = k2 * e_rem

    pw = 2 * hd
    npairs = nheads // 2

    def pairs(x):
        return jnp.stack([x[:, pw * q:pw * (q + 1)] for q in range(npairs)], axis=0)

    def all_pairs(x):
        return jnp.concatenate([pairs(rows(x, b, j)) for j in range(nchunks) for b in range(nseq)], axis=0)

    ri = lax.broadcasted_iota(jnp.int32, (1, nl, pw), 1)
    ci = lax.broadcasted_iota(jnp.int32, (1, nl, pw), 2) % hd
    strict = ci < ri
    incl = ci <= ri
    eye = (ci == ri).astype(F32)
    blockdiag = (ri // INV_BLOCK) == (ci // INV_BLOCK)
    pairmask = (lax.broadcasted_iota(jnp.int32, (1, pw, pw), 1) // hd
                == lax.broadcasted_iota(jnp.int32, (1, pw, pw), 2) // hd)

    a_p = all_pairs(a_t)
    r_p = all_pairs(r_t)
    ar = jnp.concatenate([a_p, r_p], axis=1)
    g_bk = _bdot_nt(ar, jnp.concatenate([_pair_diag(all_pairs(b_t), pairmask),
                                         _pair_diag(all_pairs(k_t), pairmask)], axis=1))
    gb, gk = g_bk[:, :, :pw], g_bk[:, :, pw:]
    a_rb = jnp.where(incl, gb[:, nl:], 0.0)
    a_ak = jnp.where(strict, gk[:, :nl], 0.0)
    a_rk = jnp.where(incl, gk[:, nl:], 0.0)
    tinv = _unit_lower_inverse(jnp.where(strict, gb[:, :nl], 0.0), eye, blockdiag, pairmask)
    v_p = all_pairs(v)
    v_d = _pair_diag(v_p, pairmask)
    akv = _bdot(a_ak, v_d)
    rkv = _bdot(a_rk, v_d)
    kpv = _bdot_tn(v_p, all_pairs(k_p))
    b_pp = all_pairs(b_p).astype(BF16)
    ta_u0 = _bdot(tinv, jnp.concatenate([_pair_diag(a_p, pairmask), _pair_diag(akv, pairmask)], axis=2))
    u0 = ta_u0[:, :, pw:]
    tar = jnp.concatenate([ta_u0[:, :, :pw], r_p], axis=1).astype(BF16)

    st = state_ref[...]
    per_chunk = nseq * npairs
    for j in range(nchunks):
        sj = slice(per_chunk * j, per_chunk * (j + 1))
        sz = _bdot_nt(tar[sj], st)
        u = sz[:, :nl] + u0[sj]
        y_j = sz[:, nl:] + _bdot(a_rb[sj], _pair_diag(u, pairmask)) + rkv[sj]
        for b in range(nseq):
            r0 = nl * (b * nchunks + j)
            for q in range(npairs):
                y_ref[r0:r0 + nl, pw * q:pw * (q + 1)] = y_j[b * npairs + q]
        decay = jnp.concatenate([pairs(e_last[(b, j)]) for b in range(nseq)], axis=0)
        st = jnp.where(pairmask, st * decay + _bdot_tn(u, b_pp[sj]) + kpv[sj], 0.0)
    state_ref[...] = st

    y = y_ref[...]
    inv_n = 1.0 / hd
    mu = segsum(y, 2) * inv_n
    d = y - mu
    var = segsum(d * d, 1) * inv_n
    yn = d * rsqrt(var + GN_EPS) * lnw_ref[...] + lnb_ref[...]
    bonus = segsum(r * k2 * rk_ref[...], 1) * v
    o_ref[...] = ((yn + bonus) * g).reshape(o_ref.shape)


def _rwkv(p_rw, w0, a0, k_k, k_a, r_k, ln_w, ln_b, wl, g2):
    bsz, s, n_rw = p_rw.shape
    c_w = w0.shape[-1]
    nchunks = min(RWKV_CHUNKS_PER_STEP, s // RWKV_CHUNK)
    nl = RWKV_CHUNK * nchunks
    nheads = c_w // RWKV_HEAD
    seg = jnp.arange(2 * LANES) // RWKV_HEAD
    ones_bd = (seg[:, None] == seg[None, :]).astype(BF16)
    pos = jnp.arange(math.gcd(bsz * nl, RWKV_CUMSUM_ROWS))
    tri = ((pos[None, :] <= pos[:, None])
           & (pos[None, :] // RWKV_CHUNK == pos[:, None] // RWKV_CHUNK)).astype(BF16)
    vec = lambda x: x.reshape(1, -1)
    return pl.pallas_call(
        functools.partial(_rwkv_kernel, nseq=bsz, nchunks=nchunks),
        grid=(s // nl,),
        in_specs=[pl.BlockSpec((bsz, nl, n_rw), lambda t: (0, t, 0))] + [_const_spec((1, c_w))] * 7
                 + [_const_spec(wl.shape), _const_spec(g2.shape), _const_spec(ones_bd.shape),
                    _const_spec(tri.shape)],
        out_specs=pl.BlockSpec((bsz, nl, c_w), lambda t: (0, t, 0)),
        out_shape=jax.ShapeDtypeStruct((bsz, s, c_w), F32),
        scratch_shapes=[pltpu.VMEM((bsz * (nheads // 2), 2 * RWKV_HEAD, 2 * RWKV_HEAD), F32),
                        pltpu.VMEM((bsz * nl, c_w), F32)],
        compiler_params=_params(("arbitrary",)),
        name="rwkv",
    )(p_rw, vec(w0), vec(a0), vec(k_k), vec(k_a), vec(r_k), vec(ln_w), vec(ln_b),
      wl, g2, ones_bd, tri)


def _swap_halves(w):
    half = w.shape[-1] // 2
    return jnp.concatenate([w[..., half:], w[..., :half]], axis=-1)


def kernel(x, c, positions, w_mod, b_mod, ffn1_norm_g, ffn1_w_gate, ffn1_w_up, ffn1_w_down, mix_norm_g, w_in, q_norm_g, w_uq, kv_norm_g, w_ukv, attn_out_norm_g, rwkv_shift_mix, rwkv_w0, rwkv_w2, rwkv_a0, rwkv_a2, rwkv_g2, rwkv_k_k, rwkv_k_a, rwkv_r_k, rwkv_ln_w, rwkv_ln_b, w_out, ffn2_norm_g, ffn2_w_gate, ffn2_w_up, ffn2_w_down, final_norm_g):
    depth = w_mod.shape[0]
    bsz, s, d = x.shape
    nh, nope, rope_w, vw = MLA_HEADS, MLA_NOPE, MLA_ROPE, MLA_V
    mla_cols = Q_LORA + KV_LORA + rope_w
    mla_w = MLA_HEADS * MLA_V

    half = rope_w // 2
    inv_freq = ROPE_BASE ** (-jnp.arange(half, dtype=F32) / half)
    sine_sign = jnp.where(jnp.arange(LANES) < 3 * half, -1.0, 1.0).astype(F32)
    cs = jnp.stack([jnp.tile(inv_freq, LANES // half), sine_sign])
    pos = positions.astype(F32)[..., None]

    h = x
    for l in range(depth):
        w_kr = w_in[l][:, Q_LORA + KV_LORA:mla_cols]
        w_in_l = jnp.concatenate([w_in[l][:, :mla_cols], _swap_halves(w_kr), w_in[l][:, mla_cols:]],
                                 axis=1).astype(BF16)
        n_mla = mla_cols + rope_w
        wq = w_uq[l].reshape(Q_LORA, nh, nope + rope_w)
        wq_rope = wq[:, :, nope:]
        wq_l = jnp.concatenate(
            [wq[:, :, :nope].reshape(Q_LORA, nh * nope),
             jnp.concatenate([wq_rope, _swap_halves(wq_rope)], axis=-1).reshape(Q_LORA, nh * 2 * rope_w)],
            axis=1).astype(BF16)
        wkv = w_ukv[l].reshape(KV_LORA, nh, nope + vw)
        wkv_l = jnp.concatenate([wkv[:, :, :nope].reshape(KV_LORA, nh * nope),
                                 wkv[:, :, nope:].reshape(KV_LORA, nh * vw)], axis=1).astype(BF16)
        c_w = rwkv_w0.shape[-1]
        zeros = jnp.zeros((DECAY_LORA, c_w), F32)
        wl = jnp.concatenate([jnp.concatenate([rwkv_w2[l], zeros], axis=1),
                              jnp.concatenate([zeros, rwkv_a2[l]], axis=1)], axis=0).astype(BF16)

        mod = _mod(c, w_mod[l], b_mod[l]).reshape(bsz, N_MOD, d)
        h = _ffn(h, mod, ffn1_norm_g[l], ffn1_w_gate[l].astype(BF16), ffn1_w_up[l].astype(BF16),
                 ffn1_w_down[l].astype(BF16), mod_rows=(0, 1, 2))
        p_rw, q, k, v = _inproj(h, mod, mix_norm_g[l], w_in_l, n_mla, pos, cs, q_norm_g[l], kv_norm_g[l],
                                wq_l, wkv_l, rwkv_shift_mix[l])
        y_a = _flash(q, k, v)
        y_b = _rwkv(p_rw, rwkv_w0[l], rwkv_a0[l], rwkv_k_k[l], rwkv_k_a[l],
                    rwkv_r_k[l].reshape(-1), rwkv_ln_w[l], rwkv_ln_b[l], wl, rwkv_g2[l].astype(BF16))
        mix = (y_a, y_b, attn_out_norm_g[l], w_out[l][:mla_w].astype(BF16), w_out[l][mla_w:].astype(BF16))
        h = _ffn(h, mod, ffn2_norm_g[l], ffn2_w_gate[l].astype(BF16), ffn2_w_up[l].astype(BF16),
                 ffn2_w_down[l].astype(BF16), mod_rows=(6, 7, 8, 5), mix=mix,
                 final_g=final_norm_g if l == depth - 1 else None)
    return h
```

```python
import functools
import math

import jax
import jax.numpy as jnp
from jax import lax
from jax.experimental import pallas as pl
from jax.experimental.pallas import tpu as pltpu

F32 = jnp.float32
BF16 = jnp.bfloat16

MLA_HEADS = 4
MLA_NOPE = 128
MLA_ROPE = 64
MLA_V = 128
Q_LORA = 384
KV_LORA = 256
ROPE_BASE = 10000.0
RWKV_HEAD = 64
DECAY_LORA = 64
AAA_LORA = 64
GATE_LORA = 128
NORM_EPS = 1e-6
GN_EPS = 64e-5
N_MOD = 9
ATTN_CHUNK = 64
LOG2_E = 1.4426950408889634
EXP_NEG_HALF = 0.6065306597126334

LANES = 128
VMEM_LIMIT_BYTES = 56 * 1024 * 1024

FFN_ROWS = 512
ATTN_CHAIN_ROWS = 1024
RWKV_CHUNK = 64
RWKV_CHUNKS_PER_STEP = 2
RWKV_CUMSUM_ROWS = 256
INV_BLOCK = 16


def _dot(a, b):
    return jnp.dot(a.astype(BF16), b.astype(BF16), preferred_element_type=F32)


def _batched(a, b, ca, cb):
    return lax.dot_general(a.astype(BF16), b.astype(BF16), (((ca,), (cb,)), ((0,), (0,))),
                           preferred_element_type=F32)


def _bdot(a, b):
    return _batched(a, b, 2, 1)


def _bdot_nt(a, b):
    return _batched(a, b, 2, 2)


def _bdot_tn(a, b):
    return _batched(a, b, 1, 1)


def _split2(x):
    hi = x.astype(BF16)
    lo = (x - hi.astype(F32)).astype(BF16)
    return hi, lo


def _split3(x):
    hi = x.astype(BF16)
    r1 = x - hi.astype(F32)
    mid = r1.astype(BF16)
    lo = (r1 - mid.astype(F32)).astype(BF16)
    return hi, mid, lo


def _sigmoid(x):
    return 1.0 / (1.0 + jnp.exp(-x))


def _rms(x, g):
    return x * lax.rsqrt(jnp.mean(x * x, axis=-1, keepdims=True) + NORM_EPS) * g


def _const_spec(shape):
    nd = len(shape)
    return pl.BlockSpec(shape, lambda *_: (0,) * nd, pipeline_mode=pl.Buffered(1))


def _params(semantics):
    return pltpu.CompilerParams(dimension_semantics=semantics, vmem_limit_bytes=VMEM_LIMIT_BYTES)


def _mod_kernel(c_ref, w_ref, b_ref, o_ref):
    c = c_ref[...]
    c_act = c * _sigmoid(c)
    o_ref[...] = _dot(c_act, w_ref[...]) + b_ref[...]


def _mod(c, w_mod, b_mod):
    bsz, d = c.shape
    n = w_mod.shape[1]
    tn = d
    return pl.pallas_call(
        _mod_kernel,
        grid=(n // tn,),
        in_specs=[pl.BlockSpec((bsz, d), lambda j: (0, 0)),
                  pl.BlockSpec((d, tn), lambda j: (0, j)),
                  pl.BlockSpec((1, tn), lambda j: (0, j))],
        out_specs=pl.BlockSpec((bsz, tn), lambda j: (0, j)),
        out_shape=jax.ShapeDtypeStruct((bsz, n), F32),
        compiler_params=_params(("arbitrary",)),
        name="mod",
    )(c, w_mod, b_mod.reshape(1, n))


def _ffn_kernel(*refs, mod_rows, pre_mix, final_norm):
    it = iter(refs)
    h_ref, mod_ref, ng_ref, wg_ref, wu_ref, wd_ref = (next(it) for _ in range(6))
    if pre_mix:
        ya_ref, yb_ref, ang_ref, woa_ref, wob_ref = (next(it) for _ in range(5))
    if final_norm:
        fng_ref = next(it)
    o_ref = next(it)

    h = h_ref[0]
    mod = mod_ref[0]
    row = lambda i: mod[i:i + 1, :]
    if pre_mix:
        ya = _rms(ya_ref[0], ang_ref[...])
        y = _dot(ya, woa_ref[...]) + _dot(yb_ref[0], wob_ref[...])
        h = h + row(mod_rows[3]) * y
    sh, sc, gt = row(mod_rows[0]), row(mod_rows[1]), row(mod_rows[2])
    u = (_rms(h, ng_ref[...]) * (1.0 + sc) + sh).astype(BF16)
    g = jnp.dot(u, wg_ref[...], preferred_element_type=F32)
    up = jnp.dot(u, wu_ref[...], preferred_element_type=F32)
    act = (g * _sigmoid(g) * up).astype(BF16)
    f = jnp.dot(act, wd_ref[...], preferred_element_type=F32)
    h = h + 0.5 * gt * f
    if final_norm:
        h = _rms(h, fng_ref[...])
    o_ref[0] = h


def _ffn(h, mod, ng, wg, wu, wd, *, mod_rows, mix=None, final_g=None):
    bsz, s, d = h.shape
    f = wg.shape[1]
    tm = min(FFN_ROWS, s)
    row_spec = lambda w: pl.BlockSpec((1, tm, w), lambda b, i: (b, i, 0))
    in_specs = [row_spec(d),
                pl.BlockSpec((1, N_MOD, d), lambda b, i: (b, 0, 0)),
                _const_spec((1, d)), _const_spec((d, f)), _const_spec((d, f)), _const_spec((f, d))]
    args = [h, mod, ng.reshape(1, d), wg, wu, wd]
    if mix is not None:
        ya, yb, ang, woa, wob = mix
        wa, wb = ya.shape[-1], yb.shape[-1]
        in_specs += [row_spec(wa), row_spec(wb), _const_spec((1, wa)),
                     _const_spec((wa, d)), _const_spec((wb, d))]
        args += [ya, yb, ang.reshape(1, wa), woa, wob]
    if final_g is not None:
        in_specs.append(_const_spec((1, d)))
        args.append(final_g.reshape(1, d))
    kern = functools.partial(_ffn_kernel, mod_rows=mod_rows, pre_mix=mix is not None,
                             final_norm=final_g is not None)
    return pl.pallas_call(
        kern,
        grid=(bsz, s // tm),
        in_specs=in_specs,
        out_specs=row_spec(d),
        out_shape=jax.ShapeDtypeStruct((bsz, s, d), F32),
        compiler_params=_params(("parallel", "parallel")),
        name="ffn_mix" if mix is not None else "ffn",
    )(*args)


def _inproj_kernel(h_ref, mod_ref, ng_ref, w_ref, pos_ref, cs_ref, qg_ref, kvg_ref, wq_ref, wkv_ref,
                   mix_ref, pr_ref, q_ref, k_ref, v_ref, last_ref, *, n_mla, scale):
    @pl.when(pl.program_id(1) == 0)
    def _():
        last_ref[...] = jnp.zeros_like(last_ref)

    mod = mod_ref[0]
    u = _rms(h_ref[0], ng_ref[...]) * (1.0 + mod[4:5, :]) + mod[3:4, :]
    proj = _dot(u, w_ref[...])
    p_rw = proj[:, n_mla:]
    tm = p_rw.shape[0]
    rowi = lax.broadcasted_iota(jnp.int32, p_rw.shape, 0)
    p_prev = jnp.where(rowi == 0, last_ref[0:1, :], pltpu.roll(p_rw, 1, axis=0))
    pr_ref[0] = p_rw + (p_prev - p_rw) * mix_ref[...]
    last_ref[0:1, :] = p_rw[tm - 1:tm, :]
    p = proj[:, :n_mla]
    nh, nope = MLA_HEADS, MLA_NOPE
    ang = pos_ref[0] * cs_ref[0:1, :]
    lane = lax.broadcasted_iota(jnp.int32, ang.shape, 1)
    low = lane < MLA_ROPE
    cs = jnp.where(low, jnp.cos(ang), cs_ref[1:2, :] * jnp.sin(ang))

    def rope(xpair):
        prod = xpair * cs
        return jnp.where(low, prod + pltpu.roll(prod, MLA_ROPE, axis=1), 0.0)

    q = _dot(_rms(p[:, :Q_LORA], qg_ref[...]), wq_ref[...]) * scale
    kv = _dot(_rms(p[:, Q_LORA:Q_LORA + KV_LORA], kvg_ref[...]), wkv_ref[...])
    k_rope = rope(p[:, Q_LORA + KV_LORA:]).astype(BF16)
    for h in range(nh):
        o = 2 * LANES * h
        q_ref[0, :, o:o + nope] = q[:, nope * h:nope * (h + 1)].astype(BF16)
        q_ref[0, :, o + nope:o + 2 * LANES] = rope(
            q[:, nh * nope + LANES * h:nh * nope + LANES * (h + 1)]).astype(BF16)
        k_ref[0, :, o:o + nope] = kv[:, nope * h:nope * (h + 1)].astype(BF16)
        k_ref[0, :, o + nope:o + 2 * LANES] = k_rope
        v_ref[0, :, o:o + MLA_V] = kv[:, nh * nope + MLA_V * h:nh * nope + MLA_V * (h + 1)].astype(BF16)
        v_ref[0, :, o + MLA_V:o + 2 * LANES] = (lane == 0).astype(BF16)


def _inproj(h, mod, ng, w, n_mla, pos, cs, qg, kvg, wq, wkv, mix):
    bsz, s, d = h.shape
    n = w.shape[1]
    tm = min(FFN_ROWS, s)
    row_spec = lambda wd_: pl.BlockSpec((1, tm, wd_), lambda b, i: (b, i, 0))
    qk_w = MLA_HEADS * 2 * LANES
    v_w = qk_w
    return pl.pallas_call(
        functools.partial(_inproj_kernel, n_mla=n_mla, scale=(MLA_NOPE + MLA_ROPE) ** -0.5 * LOG2_E),
        grid=(bsz, s // tm),
        in_specs=[row_spec(d), pl.BlockSpec((1, N_MOD, d), lambda b, i: (b, 0, 0)),
                  _const_spec((1, d)), _const_spec((d, n)), row_spec(1), _const_spec((2, LANES)),
                  _const_spec((1, Q_LORA)), _const_spec((1, KV_LORA)),
                  _const_spec(wq.shape), _const_spec(wkv.shape), _const_spec((1, n - n_mla))],
        out_specs=[row_spec(n - n_mla), row_spec(qk_w), row_spec(qk_w), row_spec(v_w)],
        out_shape=[jax.ShapeDtypeStruct((bsz, s, n - n_mla), F32),
                   jax.ShapeDtypeStruct((bsz, s, qk_w), BF16),
                   jax.ShapeDtypeStruct((bsz, s, qk_w), BF16),
                   jax.ShapeDtypeStruct((bsz, s, v_w), BF16)],
        scratch_shapes=[pltpu.VMEM((8, n - n_mla), F32)],
        compiler_params=_params(("parallel", "arbitrary")),
        name="inproj",
    )(h, mod, ng.reshape(1, d), w, pos, cs, qg.reshape(1, -1), kvg.reshape(1, -1), wq, wkv,
      mix.reshape(1, -1))


def _flash_kernel(q_ref, k_ref, v_ref, o_ref, *, bq):
    bk = 2 * bq
    step = pl.program_id(2)
    pl.when(step == 0)(functools.partial(_flash_step, q_ref, k_ref, v_ref, o_ref, 0, bq=bq))
    pl.when(step > 0)(functools.partial(_flash_step, q_ref, k_ref, v_ref, o_ref, step, bq=bq))


def _flash_step(q_ref, k_ref, v_ref, o_ref, i, *, bq):
    bk = 2 * bq
    base = i * bk if isinstance(i, int) else pl.multiple_of(i * bk, bk)

    def scores(chain, start, n):
        kb = k_ref[0, pl.ds(start, n), :]
        return lax.dot_general(q_ref[0, chain * bq:(chain + 1) * bq, :], kb, (((1,), (1,)), ((), ())),
                               preferred_element_type=F32)

    def first(s, start, n):
        m = jnp.max(s, axis=-1, keepdims=True)
        p = jnp.exp2((s - m).astype(BF16))
        return m, jnp.dot(p, v_ref[0, pl.ds(start, n), :], preferred_element_type=F32)

    def update(carry, s, start, n):
        m, acc = carry
        m_new = jnp.maximum(m, jnp.max(s, axis=-1, keepdims=True))
        alpha = jnp.exp2(m - m_new)
        p = jnp.exp2((s - m_new).astype(BF16))
        return m_new, alpha * acc + jnp.dot(p, v_ref[0, pl.ds(start, n), :], preferred_element_type=F32)

    visible = (lax.broadcasted_iota(jnp.int32, (bq, bq), 1) // ATTN_CHUNK
               <= lax.broadcasted_iota(jnp.int32, (bq, bq), 0) // ATTN_CHUNK)
    mid = base + bq if isinstance(i, int) else pl.multiple_of(base + bq, bq)
    s_a = jnp.where(visible, scores(0, base, bq), -1e30)
    s_b = jnp.where(visible, scores(1, mid, bq), -1e30)
    s_b0 = scores(1, base, bq)
    chain_a = first(s_a, base, bq)
    chain_b = update(first(s_b, mid, bq), s_b0, base, bq)

    def block(j, carry):
        chain_a, chain_b = carry
        start = pl.multiple_of(j * bk, bk)
        s_a = scores(0, start, bk)
        s_b = scores(1, start, bk)
        return update(chain_a, s_a, start, bk), update(chain_b, s_b, start, bk)

    if not (isinstance(i, int) and i == 0):
        pair = lambda jj, carry: block(2 * jj + 1, block(2 * jj, carry))
        carry = lax.fori_loop(0, i // 2, pair, (chain_a, chain_b))
        chain_a, chain_b = lax.fori_loop(0, i % 2, lambda _, carry: block(i - 1, carry), carry)
    for chain, (_, acc) in enumerate((chain_a, chain_b)):
        o_ref[0, chain * bq:(chain + 1) * bq, :] = acc[:, :MLA_V] / acc[:, MLA_V:MLA_V + 1]


def _flash(q, k, v):
    bsz, s, _ = q.shape
    bq = min(ATTN_CHAIN_ROWS, s // 2)
    qk_w = 2 * LANES
    return pl.pallas_call(
        functools.partial(_flash_kernel, bq=bq),
        grid=(bsz, MLA_HEADS, s // (2 * bq)),
        in_specs=[pl.BlockSpec((1, 2 * bq, qk_w), lambda b, h, i: (b, i, h)),
                  pl.BlockSpec((1, s, qk_w), lambda b, h, i: (b, 0, h)),
                  pl.BlockSpec((1, s, qk_w), lambda b, h, i: (b, 0, h))],
        out_specs=pl.BlockSpec((1, 2 * bq, MLA_V), lambda b, h, i: (b, i, h)),
        out_shape=jax.ShapeDtypeStruct((bsz, s, MLA_HEADS * MLA_V), F32),
        compiler_params=_params(("parallel", "parallel", "arbitrary")),
        name="flash",
    )(q, k, v)


def _pair_diag(y, pairmask):
    yb = y.astype(BF16)
    return jnp.where(pairmask, jnp.concatenate([yb, yb], axis=1), jnp.zeros((), BF16))


def _unit_lower_inverse(a, eye, blockdiag, pairmask):
    pw = a.shape[-1]
    diag = lambda y: _pair_diag(y, pairmask)
    mm = lambda x, y: _bdot(x, diag(y))

    def mm2(x, y1, y2):
        both = _bdot(x, jnp.concatenate([diag(y1), diag(y2)], axis=2))
        return both[:, :, :pw], both[:, :, pw:]

    ad = jnp.where(blockdiag, a, 0.0)
    ao = a - ad
    a2 = mm(ad, ad)
    a4, a23 = mm2(a2, a2, eye + ad)
    t1 = eye + ad + a23
    a8, a4t1 = mm2(a4, a4, t1)
    t2 = t1 + a4t1
    td = t2 + mm(a8, t2)
    n1 = mm(td, ao)
    n1td, n2 = mm2(n1, td, n1)
    w = td + n1td
    return w + mm(n2, w)


def _rwkv_kernel(p_ref, w0_ref, a0_ref, kk_ref, ka_ref, rk_ref, lnw_ref, lnb_ref,
                 wl_ref, g2_ref, ones_ref, tri_ref, o_ref, state_ref, y_ref, *, nseq, nchunks):
    t = pl.program_id(0)
    nl = RWKV_CHUNK
    hd = RWKV_HEAD
    c_w = o_ref.shape[-1]
    nheads = c_w // hd

    @pl.when(t == 0)
    def _():
        state_ref[...] = jnp.zeros_like(state_ref)

    p = p_ref[...].reshape(nseq * nchunks * nl, p_ref.shape[-1])

    r = p[:, :c_w]
    k = p[:, c_w:2 * c_w]
    v = p[:, 2 * c_w:3 * c_w]
    xwa = p[:, 3 * c_w:3 * c_w + LANES]
    xg = p[:, 3 * c_w + LANES:]

    def segsum(x, terms):
        halves = []
        for o in range(0, c_w, 2 * LANES):
            xo = x[:, o:o + 2 * LANES]
            pieces = _split2(xo) if terms == 2 else (xo.astype(BF16),)
            halves.append(sum(jnp.dot(pc, ones_ref[...], preferred_element_type=F32) for pc in pieces))
        return jnp.concatenate(halves, axis=1)

    def rsqrt(x):
        return jnp.exp2(-0.5 * jnp.log2(x))

    def sigmoid(x):
        return 0.5 * jnp.tanh(0.5 * x) + 0.5

    lane = lax.broadcasted_iota(jnp.int32, xwa.shape, 1)
    wa = _dot(jnp.where(lane < DECAY_LORA, jnp.tanh(xwa), xwa), wl_ref[...])
    z = w0_ref[...] + wa[:, :c_w]
    lw = -EXP_NEG_HALF * sigmoid(z)
    a = sigmoid(a0_ref[...] + wa[:, c_w:])
    g = _dot(sigmoid(xg), g2_ref[...])
    kk = k * kk_ref[...]
    kk = kk * rsqrt(jnp.maximum(segsum(kk * kk, 1), 1e-24))
    k2 = k * (1.0 + (a - 1.0) * ka_ref[...])
    bb = kk * a

    tri = tri_ref[...]
    grp = tri.shape[0]
    parts = _split3(lw)
    c = jnp.concatenate(
        [sum(jnp.dot(tri, part[o:o + grp], preferred_element_type=F32) for part in parts)
         for o in range(0, lw.shape[0], grp)], axis=0)
    slots = [(b, j) for b in range(nseq) for j in range(nchunks)]
    rows = lambda x, b, j: x[nl * (b * nchunks + j):nl * (b * nchunks + j + 1)]
    e_last = {bj: jnp.exp(rows(c, *bj)[nl - 1:nl, :]) for bj in slots}
    e_neg = jnp.exp(-c)
    e_rem = jnp.concatenate([e_last[bj] * rows(e_neg, *bj) for bj in slots], axis=0)
    a_t = -kk * jnp.exp(c - lw)
    r_t = r * jnp.exp(c)
    b_t = bb * e_neg
    k_t = k2 * e_neg
    b_p = bb * e_rem
    k_p = k2 * e_rem

    pw = 2 * hd
    npairs = nheads // 2

    def pairs(x):
        return jnp.stack([x[:, pw * q:pw * (q + 1)] for q in range(npairs)], axis=0)

    def all_pairs(x):
        return jnp.concatenate([pairs(rows(x, b, j)) for j in range(nchunks) for b in range(nseq)], axis=0)

    ri = lax.broadcasted_iota(jnp.int32, (1, nl, pw), 1)
    ci = lax.broadcasted_iota(jnp.int32, (1, nl, pw), 2) % hd
    strict = ci < ri
    incl = ci <= ri
    eye = (ci == ri).astype(F32)
    blockdiag = (ri // INV_BLOCK) == (ci // INV_BLOCK)
    pairmask = (lax.broadcasted_iota(jnp.int32, (1, pw, pw), 1) // hd
                == lax.broadcasted_iota(jnp.int32, (1, pw, pw), 2) // hd)

    a_p = all_pairs(a_t)
    r_p = all_pairs(r_t)
    ar = jnp.concatenate([a_p, r_p], axis=1)
    g_bk = _bdot_nt(ar, jnp.concatenate([_pair_diag(all_pairs(b_t), pairmask),
                                         _pair_diag(all_pairs(k_t), pairmask)], axis=1))
    gb, gk = g_bk[:, :, :pw], g_bk[:, :, pw:]
    a_rb = jnp.where(incl, gb[:, nl:], 0.0)
    a_ak = jnp.where(strict, gk[:, :nl], 0.0)
    a_rk = jnp.where(incl, gk[:, nl:], 0.0)
    tinv = _unit_lower_inverse(jnp.where(strict, gb[:, :nl], 0.0), eye, blockdiag, pairmask)
    v_p = all_pairs(v)
    v_d = _pair_diag(v_p, pairmask)
    akv = _bdot(a_ak, v_d)
    rkv = _bdot(a_rk, v_d)
    kpv = _bdot_tn(v_p, all_pairs(k_p))
    b_pp = all_pairs(b_p).astype(BF16)
    ta_u0 = _bdot(tinv, jnp.concatenate([_pair_diag(a_p, pairmask), _pair_diag(akv, pairmask)], axis=2))
    u0 = ta_u0[:, :, pw:]
    tar = jnp.concatenate([ta_u0[:, :, :pw], r_p], axis=1).astype(BF16)

    st = state_ref[...]
    per_chunk = nseq * npairs
    for j in range(nchunks):
        sj = slice(per_chunk * j, per_chunk * (j + 1))
        sz = _bdot_nt(tar[sj], st)
        u = sz[:, :nl] + u0[sj]
        y_j = sz[:, nl:] + _bdot(a_rb[sj], _pair_diag(u, pairmask)) + rkv[sj]
        for b in range(nseq):
            r0 = nl * (b * nchunks + j)
            for q in range(npairs):
                y_ref[r0:r0 + nl, pw * q:pw * (q + 1)] = y_j[b * npairs + q]
        decay = jnp.concatenate([pairs(e_last[(b, j)]) for b in range(nseq)], axis=0)
        st = jnp.where(pairmask, st * decay + _bdot_tn(u, b_pp[sj]) + kpv[sj], 0.0)
    state_ref[...] = st

    y = y_ref[...]
    inv_n = 1.0 / hd
    mu = segsum(y, 2) * inv_n
    d = y - mu
    var = segsum(d * d, 1) * inv_n
    yn = d * rsqrt(var + GN_EPS) * lnw_ref[...] + lnb_ref[...]
    bonus = segsum(r * k2 * rk_ref[...], 1) * v
    o_ref[...] = ((yn + bonus) * g).reshape(o_ref.shape)


def _rwkv(p_rw, w0, a0, k_k, k_a, r_k, ln_w, ln_b, wl, g2):
    bsz, s, n_rw = p_rw.shape
    c_w = w0.shape[-1]
    nchunks = min(RWKV_CHUNKS_PER_STEP, s // RWKV_CHUNK)
    nl = RWKV_CHUNK * nchunks
    nheads = c_w // RWKV_HEAD
    seg = jnp.arange(2 * LANES) // RWKV_HEAD
    ones_bd = (seg[:, None] == seg[None, :]).astype(BF16)
    pos = jnp.arange(math.gcd(bsz * nl, RWKV_CUMSUM_ROWS))
    tri = ((pos[None, :] <= pos[:, None])
           & (pos[None, :] // RWKV_CHUNK == pos[:, None] // RWKV_CHUNK)).astype(BF16)
    vec = lambda x: x.reshape(1, -1)
    return pl.pallas_call(
        functools.partial(_rwkv_kernel, nseq=bsz, nchunks=nchunks),
        grid=(s // nl,),
        in_specs=[pl.BlockSpec((bsz, nl, n_rw), lambda t: (0, t, 0))] + [_const_spec((1, c_w))] * 7
                 + [_const_spec(wl.shape), _const_spec(g2.shape), _const_spec(ones_bd.shape),
                    _const_spec(tri.shape)],
        out_specs=pl.BlockSpec((bsz, nl, c_w), lambda t: (0, t, 0)),
        out_shape=jax.ShapeDtypeStruct((bsz, s, c_w), F32),
        scratch_shapes=[pltpu.VMEM((bsz * (nheads // 2), 2 * RWKV_HEAD, 2 * RWKV_HEAD), F32),
                        pltpu.VMEM((bsz * nl, c_w), F32)],
        compiler_params=_params(("arbitrary",)),
        name="rwkv",
    )(p_rw, vec(w0), vec(a0), vec(k_k), vec(k_a), vec(r_k), vec(ln_w), vec(ln_b),
      wl, g2, ones_bd, tri)


def _swap_halves(w):
    half = w.shape[-1] // 2
    return jnp.concatenate([w[..., half:], w[..., :half]], axis=-1)


def kernel(x, c, positions, w_mod, b_mod, ffn1_norm_g, ffn1_w_gate, ffn1_w_up, ffn1_w_down, mix_norm_g, w_in, q_norm_g, w_uq, kv_norm_g, w_ukv, attn_out_norm_g, rwkv_shift_mix, rwkv_w0, rwkv_w2, rwkv_a0, rwkv_a2, rwkv_g2, rwkv_k_k, rwkv_k_a, rwkv_r_k, rwkv_ln_w, rwkv_ln_b, w_out, ffn2_norm_g, ffn2_w_gate, ffn2_w_up, ffn2_w_down, final_norm_g):
    depth = w_mod.shape[0]
    bsz, s, d = x.shape
    nh, nope, rope_w, vw = MLA_HEADS, MLA_NOPE, MLA_ROPE, MLA_V
    mla_cols = Q_LORA + KV_LORA + rope_w
    mla_w = MLA_HEADS * MLA_V

    half = rope_w // 2
    inv_freq = ROPE_BASE ** (-jnp.arange(half, dtype=F32) / half)
    sine_sign = jnp.where(jnp.arange(LANES) < 3 * half, -1.0, 1.0).astype(F32)
    cs = jnp.stack([jnp.tile(inv_freq, LANES // half), sine_sign])
    pos = positions.astype(F32)[..., None]

    h = x
    for l in range(depth):
        w_kr = w_in[l][:, Q_LORA + KV_LORA:mla_cols]
        w_in_l = jnp.concatenate([w_in[l][:, :mla_cols], _swap_halves(w_kr), w_in[l][:, mla_cols:]],
                                 axis=1).astype(BF16)
        n_mla = mla_cols + rope_w
        wq = w_uq[l].reshape(Q_LORA, nh, nope + rope_w)
        wq_rope = wq[:, :, nope:]
        wq_l = jnp.concatenate(
            [wq[:, :, :nope].reshape(Q_LORA, nh * nope),
             jnp.concatenate([wq_rope, _swap_halves(wq_rope)], axis=-1).reshape(Q_LORA, nh * 2 * rope_w)],
            axis=1).astype(BF16)
        wkv = w_ukv[l].reshape(KV_LORA, nh, nope + vw)
        wkv_l = jnp.concatenate([wkv[:, :, :nope].reshape(KV_LORA, nh * nope),
                                 wkv[:, :, nope:].reshape(KV_LORA, nh * vw)], axis=1).astype(BF16)
        c_w = rwkv_w0.shape[-1]
        zeros = jnp.zeros((DECAY_LORA, c_w), F32)
        wl = jnp.concatenate([jnp.concatenate([rwkv_w2[l], zeros], axis=1),
                              jnp.concatenate([zeros, rwkv_a2[l]], axis=1)], axis=0).astype(BF16)

        mod = _mod(c, w_mod[l], b_mod[l]).reshape(bsz, N_MOD, d)
        h = _ffn(h, mod, ffn1_norm_g[l], ffn1_w_gate[l].astype(BF16), ffn1_w_up[l].astype(BF16),
                 ffn1_w_down[l].astype(BF16), mod_rows=(0, 1, 2))
        p_rw, q, k, v = _inproj(h, mod, mix_norm_g[l], w_in_l, n_mla, pos, cs, q_norm_g[l], kv_norm_g[l],
                                wq_l, wkv_l, rwkv_shift_mix[l])
        y_a = _flash(q, k, v)
        y_b = _rwkv(p_rw, rwkv_w0[l], rwkv_a0[l], rwkv_k_k[l], rwkv_k_a[l],
                    rwkv_r_k[l].reshape(-1), rwkv_ln_w[l], rwkv_ln_b[l], wl, rwkv_g2[l].astype(BF16))
        mix = (y_a, y_b, attn_out_norm_g[l], w_out[l][:mla_w].astype(BF16), w_out[l][mla_w:].astype(BF16))
        h = _ffn(h, mod, ffn2_norm_g[l], ffn2_w_gate[l].astype(BF16), ffn2_w_up[l].astype(BF16),
                 ffn2_w_down[l].astype(BF16), mod_rows=(6, 7, 8, 5), mix=mix,
                 final_g=final_norm_g if l == depth - 1 else None)
    return h
```
